```python
import jax, jax.numpy as jnp
from jax import lax
import numpy as np

D_MODEL = 4096
BATCH = 4
SEQ = 2048
DEPTH = 1

HEAD_DIM = 128
SWA_Q_HEADS = 16
SWA_KV_HEADS = 4
SWA_WINDOW = 128
SWA_BLOCK = 128
ROPE_THETA = 10000.0
NAT_HEADS = 16
NAT_KR_MAX = 8
NAT_KC = 16
GRID_W = 64
NAT_QROWS = 2
SWA_Q_W = SWA_Q_HEADS * HEAD_DIM
SWA_KV_W = SWA_KV_HEADS * HEAD_DIM
NAT_W = NAT_HEADS * HEAD_DIM
MIX_W = SWA_Q_W + NAT_W
IN_W = SWA_Q_W + 2 * SWA_KV_W + 3 * NAT_W
MEM_LEN = 256
MEM_HEADS = 4
MEM_HEAD_DIM = 256
MEM_W = MEM_HEADS * MEM_HEAD_DIM
PEER_HEADS = 8
PEER_NKEYS = 128
PEER_EXPERTS = PEER_NKEYS * PEER_NKEYS
PEER_QDIM = 256
PEER_TOPK = 16
PEER_CHUNK = 64
RMS_EPS = 1e-6
NEG_INF = -1e30

kernel_name = 'hymba_swa_nat_peer_encoder_layer'


def rms_norm(x, g):
    xf = x.astype(jnp.float32)
    y = xf * lax.rsqrt(jnp.mean(xf * xf, axis=-1, keepdims=True) + RMS_EPS)
    return (y * g.astype(jnp.float32)).astype(x.dtype)


def rope(x, pos):
    half = HEAD_DIM // 2
    inv = ROPE_THETA ** (-jnp.arange(half, dtype=jnp.float32) / half)
    ang = pos.astype(jnp.float32)[:, None] * inv[None, :]
    cos = jnp.cos(ang)[None, :, None, :]
    sin = jnp.sin(ang)[None, :, None, :]
    xf = x.astype(jnp.float32)
    x1, x2 = xf[..., :half], xf[..., half:]
    return jnp.concatenate([x1 * cos - x2 * sin, x2 * cos + x1 * sin], axis=-1).astype(x.dtype)


def swa_sink_attention(q, k, v, sink):
    B, S = q.shape[0], q.shape[1]
    nb = S // SWA_BLOCK
    G = SWA_Q_HEADS // SWA_KV_HEADS
    qb = q.reshape(B, nb, SWA_BLOCK, SWA_KV_HEADS, G, HEAD_DIM)
    pad = ((0, 0), (SWA_BLOCK, SWA_BLOCK), (0, 0), (0, 0))

    def band(t):
        tp = jnp.pad(t, pad).reshape(B, nb + 2, SWA_BLOCK, SWA_KV_HEADS, HEAD_DIM)
        return jnp.concatenate([tp[:, :-2], tp[:, 1:-1], tp[:, 2:]], axis=2)

    kb, vb = band(k), band(v)
    s = jnp.einsum('bnqhgd,bnkhd->bnhgqk', qb, kb, preferred_element_type=jnp.float32) * (HEAD_DIM ** -0.5)
    qpos = np.arange(nb)[:, None] * SWA_BLOCK + np.arange(SWA_BLOCK)[None, :]
    kpos = np.arange(nb)[:, None] * SWA_BLOCK - SWA_BLOCK + np.arange(3 * SWA_BLOCK)[None, :]
    valid = ((np.abs(qpos[:, :, None] - kpos[:, None, :]) <= SWA_WINDOW)
             & (kpos[:, None, :] >= 0) & (kpos[:, None, :] < S))
    s = jnp.where(valid[None, :, None, None], s, NEG_INF)
    sink_l = sink.astype(jnp.float32).reshape(SWA_KV_HEADS, G)[None, None, :, :, None, None]
    m = jnp.maximum(jnp.max(s, axis=-1, keepdims=True), sink_l)
    p = jnp.exp(s - m)
    p = p / (jnp.sum(p, axis=-1, keepdims=True) + jnp.exp(sink_l - m))
    o = jnp.einsum('bnhgqk,bnkhd->bnqhgd', p.astype(v.dtype), vb)
    return o.reshape(B, S, SWA_Q_W)


def neighborhood_attention(q, k, v, rel_bias):
    B, S = q.shape[0], q.shape[1]
    rows = S // GRID_W
    kr = min(NAT_KR_MAX, rows)
    span = min(rows, NAT_QROWS + kr - 1)
    nb = rows // NAT_QROWS
    qbl = NAT_QROWS * GRID_W
    r0 = np.arange(nb) * NAT_QROWS
    kstart = np.clip(r0 - kr // 2, 0, rows - span)
    key_rows = kstart[:, None] + np.arange(span)[None, :]
    kg = k.reshape(B, rows, GRID_W, NAT_HEADS, HEAD_DIM)[:, key_rows].reshape(B, nb, span * GRID_W, NAT_HEADS, HEAD_DIM)
    vg = v.reshape(B, rows, GRID_W, NAT_HEADS, HEAD_DIM)[:, key_rows].reshape(B, nb, span * GRID_W, NAT_HEADS, HEAD_DIM)
    qb = q.reshape(B, nb, qbl, NAT_HEADS, HEAD_DIM)
    q_row = r0[:, None] + (np.arange(qbl) // GRID_W)[None, :]
    q_col = np.broadcast_to((np.arange(qbl) % GRID_W)[None, :], q_row.shape)
    k_row = np.repeat(key_rows, GRID_W, axis=1)
    k_col = np.broadcast_to(np.tile(np.arange(GRID_W), span)[None, :], k_row.shape)
    rs = np.clip(q_row - kr // 2, 0, rows - kr)[:, :, None]
    cs = np.clip(q_col - NAT_KC // 2, 0, GRID_W - NAT_KC)[:, :, None]
    kr3, kc3 = k_row[:, None, :], k_col[:, None, :]
    valid = (kr3 >= rs) & (kr3 < rs + kr) & (kc3 >= cs) & (kc3 < cs + NAT_KC)
    dr = np.clip(kr3 - q_row[:, :, None] + NAT_KR_MAX - 1, 0, 2 * NAT_KR_MAX - 2)
    dc = np.clip(kc3 - q_col[:, :, None] + NAT_KC - 1, 0, 2 * NAT_KC - 2)
    bias = jnp.transpose(rel_bias[:, dr, dc], (1, 0, 2, 3)).astype(jnp.float32)
    s = jnp.einsum('bnqhd,bnkhd->bnhqk', qb, kg, preferred_element_type=jnp.float32) * (HEAD_DIM ** -0.5)
    s = jnp.where(valid[None, :, None], s + bias[None], NEG_INF)
    p = jax.nn.softmax(s, axis=-1)
    o = jnp.einsum('bnhqk,bnkhd->bnqhd', p.astype(v.dtype), vg)
    return o.reshape(B, S, NAT_W)


def memory_cross_attention(hn, memn, w_q, w_k, w_v, w_o):
    B, S = hn.shape[0], hn.shape[1]
    M = memn.shape[1]
    q = (hn @ w_q).reshape(B, S, MEM_HEADS, MEM_HEAD_DIM)
    k = (memn @ w_k).reshape(B, M, MEM_HEADS, MEM_HEAD_DIM)
    v = (memn @ w_v).reshape(B, M, MEM_HEADS, MEM_HEAD_DIM)
    s = jnp.einsum('bqhd,bkhd->bhqk', q, k, preferred_element_type=jnp.float32) * (MEM_HEAD_DIM ** -0.5)
    p = jax.nn.softmax(s, axis=-1)
    o = jnp.einsum('bhqk,bkhd->bqhd', p.astype(v.dtype), v).reshape(B, S, MEM_W)
    return o @ w_o


def peer_ffn(hn, w_q, sub_keys, u, v):
    B, S, D = hn.shape
    T = B * S
    x2 = hn.reshape(T, D)
    q = (x2 @ w_q).reshape(T, PEER_HEADS, 2, PEER_QDIM // 2)
    s = jnp.einsum('thcd,hckd->thck', q, sub_keys, preferred_element_type=jnp.float32)
    s1, i1 = lax.top_k(s[:, :, 0], PEER_TOPK)
    s2, i2 = lax.top_k(s[:, :, 1], PEER_TOPK)
    cand_s = (s1[..., :, None] + s2[..., None, :]).reshape(T, PEER_HEADS, PEER_TOPK * PEER_TOPK)
    cand_i = (i1[..., :, None] * PEER_NKEYS + i2[..., None, :]).reshape(T, PEER_HEADS, PEER_TOPK * PEER_TOPK)
    top_s, pos = lax.top_k(cand_s, PEER_TOPK)
    idx = jnp.take_along_axis(cand_i, pos, axis=-1).reshape(T, PEER_HEADS * PEER_TOPK)
    g = jax.nn.softmax(top_s, axis=-1).reshape(T, PEER_HEADS * PEER_TOPK)
    nc = T // PEER_CHUNK

    def chunk(args):
        xc, ic, gc = args
        a = jnp.einsum('cd,ced->ce', xc, u[ic], preferred_element_type=jnp.float32)
        w = (jax.nn.gelu(a, approximate=False) * gc).astype(v.dtype)
        return jnp.einsum('ce,ced->cd', w, v[ic])

    out = lax.map(chunk, (x2.reshape(nc, PEER_CHUNK, D),
                          idx.reshape(nc, PEER_CHUNK, -1),
                          g.reshape(nc, PEER_CHUNK, -1)))
    return out.reshape(B, S, D).astype(hn.dtype)


def setup_inputs(seed: int = 0) -> dict:
    key = jax.random.key(seed)
    ks = jax.random.split(key, 24)
    f32 = jnp.float32

    def nrm(k, shape, scale):
        return jax.random.normal(k, shape, f32) * scale

    def gain(k, shape):
        return 1.0 + 0.1 * jax.random.normal(k, shape, f32)

    return {
        'x': nrm(ks[0], (BATCH, SEQ, D_MODEL), 1.0),
        'mem': nrm(ks[1], (BATCH, MEM_LEN, D_MODEL), 1.0),
        'norm_mix': gain(ks[2], (DEPTH, D_MODEL)),
        'w_in': nrm(ks[3], (DEPTH, D_MODEL, IN_W), D_MODEL ** -0.5),
        'swa_sink': nrm(ks[4], (DEPTH, SWA_Q_HEADS), 1.0),
        'nat_rel_bias': nrm(ks[5], (DEPTH, NAT_HEADS, 2 * NAT_KR_MAX - 1, 2 * NAT_KC - 1), 0.5),
        'out_norm_swa': gain(ks[6], (DEPTH, SWA_Q_W)),
        'out_norm_nat': gain(ks[7], (DEPTH, NAT_W)),
        'w_out': nrm(ks[8], (DEPTH, MIX_W, D_MODEL), MIX_W ** -0.5),
        'norm_mem_q': gain(ks[9], (DEPTH, D_MODEL)),
        'norm_mem_kv': gain(ks[10], (DEPTH, D_MODEL)),
        'w_mem_q': nrm(ks[11], (DEPTH, D_MODEL, MEM_W), D_MODEL ** -0.5),
        'w_mem_k': nrm(ks[12], (DEPTH, D_MODEL, MEM_W), D_MODEL ** -0.5),
        'w_mem_v': nrm(ks[13], (DEPTH, D_MODEL, MEM_W), D_MODEL ** -0.5),
        'w_mem_o': nrm(ks[14], (DEPTH, MEM_W, D_MODEL), MEM_W ** -0.5),
        'norm_ffn': gain(ks[15], (DEPTH, D_MODEL)),
        'w_peer_q': nrm(ks[16], (DEPTH, D_MODEL, PEER_HEADS * PEER_QDIM), D_MODEL ** -0.5),
        'peer_sub_keys': nrm(ks[17], (DEPTH, PEER_HEADS, 2, PEER_NKEYS, PEER_QDIM // 2), (PEER_QDIM // 2) ** -0.5),
        'peer_u': nrm(ks[18], (DEPTH, PEER_EXPERTS, D_MODEL), D_MODEL ** -0.5),
        'peer_v': nrm(ks[19], (DEPTH, PEER_EXPERTS, D_MODEL), PEER_TOPK ** -0.5),
        'norm_final': gain(ks[20], (D_MODEL,)),
    }


def reference(x, mem, norm_mix, w_in, swa_sink, nat_rel_bias, out_norm_swa, out_norm_nat, w_out,
              norm_mem_q, norm_mem_kv, w_mem_q, w_mem_k, w_mem_v, w_mem_o,
              norm_ffn, w_peer_q, peer_sub_keys, peer_u, peer_v, norm_final):
    B, S = x.shape[0], x.shape[1]
    pos = jnp.arange(S, dtype=jnp.int32)
    splits = [SWA_Q_W, SWA_Q_W + SWA_KV_W, SWA_Q_W + 2 * SWA_KV_W,
              SWA_Q_W + 2 * SWA_KV_W + NAT_W, SWA_Q_W + 2 * SWA_KV_W + 2 * NAT_W]
    h = x
    for l in range(DEPTH):
        xn = rms_norm(h, norm_mix[l])
        qa, ka, va, qb, kb, vb = jnp.split(xn @ w_in[l], splits, axis=-1)
        qa = rope(qa.reshape(B, S, SWA_Q_HEADS, HEAD_DIM), pos)
        ka = rope(ka.reshape(B, S, SWA_KV_HEADS, HEAD_DIM), pos)
        va = va.reshape(B, S, SWA_KV_HEADS, HEAD_DIM)
        oa = swa_sink_attention(qa, ka, va, swa_sink[l])
        ob = neighborhood_attention(qb.reshape(B, S, NAT_HEADS, HEAD_DIM),
                                    kb.reshape(B, S, NAT_HEADS, HEAD_DIM),
                                    vb.reshape(B, S, NAT_HEADS, HEAD_DIM), nat_rel_bias[l])
        mixed = jnp.concatenate([rms_norm(oa, out_norm_swa[l]), rms_norm(ob, out_norm_nat[l])], axis=-1)
        h = h + mixed @ w_out[l]
        h = h + memory_cross_attention(rms_norm(h, norm_mem_q[l]), rms_norm(mem, norm_mem_kv[l]),
                                       w_mem_q[l], w_mem_k[l], w_mem_v[l], w_mem_o[l])
        h = h + peer_ffn(rms_norm(h, norm_ffn[l]), w_peer_q[l], peer_sub_keys[l], peer_u[l], peer_v[l])
    return rms_norm(h, norm_final)
```

```python
import functools

import numpy as np
import jax
import jax.numpy as jnp
from jax import lax
from jax.experimental import pallas as pl
from jax.experimental.pallas import tpu as pltpu

F32 = jnp.float32
BF16 = jnp.bfloat16

D_MODEL = 4096
HEAD_DIM = 128
SWA_Q_HEADS = 16
SWA_KV_HEADS = 4
SWA_GROUP = SWA_Q_HEADS // SWA_KV_HEADS
SWA_WINDOW = 128
SWA_BLOCK = 128
ROPE_THETA = 10000.0
NAT_HEADS = 16
NAT_KR_MAX = 8
NAT_KC = 16
GRID_W = 64
NAT_QROWS = 2
SWA_Q_W = SWA_Q_HEADS * HEAD_DIM
SWA_KV_W = SWA_KV_HEADS * HEAD_DIM
NAT_W = NAT_HEADS * HEAD_DIM
MEM_HEADS = 4
MEM_HEAD_DIM = 256
MEM_W = MEM_HEADS * MEM_HEAD_DIM
PEER_HEADS = 8
PEER_NKEYS = 128
PEER_QDIM = 256
PEER_TOPK = 16
RMS_EPS = 1e-6
NEG_INF = -1e30

LANES = 128
VMEM_LIMIT_BYTES = 56 * 1024 * 1024

_CELLS = [(a, b) for a in range(PEER_TOPK) for b in range(PEER_TOPK) if (a + 1) * (b + 1) <= PEER_TOPK]
_N_CELLS = len(_CELLS)
_CELL_ROWS = -(-_N_CELLS // 8) * 8


def _params(sem):
    return pltpu.CompilerParams(dimension_semantics=sem, vmem_limit_bytes=VMEM_LIMIT_BYTES)


def _rms_matmul_kernel(*refs, seg_widths, norm, has_res, n_rope_blocks, emit_xn):
    n_seg = len(seg_widths)
    pos = 0
    x_refs = refs[pos:pos + n_seg]; pos += n_seg
    g_refs = ()
    if norm:
        g_refs = refs[pos:pos + n_seg]; pos += n_seg
    w_ref = refs[pos]; pos += 1
    res_ref = None
    if has_res:
        res_ref = refs[pos]; pos += 1
    cos_ref = sin_ref = None
    if n_rope_blocks:
        cos_ref, sin_ref = refs[pos], refs[pos + 1]; pos += 2
    o_ref = refs[pos]; pos += 1
    xn_out_ref = None
    if emit_xn:
        xn_out_ref = refs[pos]; pos += 1
    xn_ref = refs[pos]

    j = pl.program_id(1)

    @pl.when(j == 0)
    def _():
        off = 0
        for s in range(n_seg):
            x = x_refs[s][...].astype(F32)
            if norm:
                ms = jnp.mean(x * x, axis=-1, keepdims=True)
                x = (x * lax.rsqrt(ms + RMS_EPS)) * g_refs[s][...]
            xn_ref[:, off:off + seg_widths[s]] = x.astype(BF16)
            off += seg_widths[s]
        if emit_xn:
            xn_out_ref[...] = xn_ref[...]

    acc = jnp.dot(xn_ref[...], w_ref[...], preferred_element_type=F32)
    if has_res:
        acc = acc + res_ref[...]

    if n_rope_blocks:
        @pl.when(j < n_rope_blocks)
        def _():
            cos = cos_ref[...]
            sin = sin_ref[...]
            for g in range(acc.shape[1] // HEAD_DIM):
                blk = acc[:, g * HEAD_DIM:(g + 1) * HEAD_DIM]
                rot = pltpu.roll(blk, HEAD_DIM // 2, 1)
                o_ref[:, g * HEAD_DIM:(g + 1) * HEAD_DIM] = (blk * cos + rot * sin).astype(o_ref.dtype)

        @pl.when(j >= n_rope_blocks)
        def _():
            o_ref[...] = acc.astype(o_ref.dtype)
    else:
        o_ref[...] = acc.astype(o_ref.dtype)


def _rms_matmul(xs, gains, w, *, bm, bn, out_dtype, res=None, rope=None, n_rope_blocks=0, emit_xn=False,
                name):
    T = xs[0].shape[0]
    seg_widths = tuple(int(x.shape[1]) for x in xs)
    K = sum(seg_widths)
    N = w.shape[1]
    norm = gains is not None
    assert w.shape[0] == K and T % bm == 0 and N % bn == 0
    in_specs = [pl.BlockSpec((bm, kw), lambda i, j: (i, 0)) for kw in seg_widths]
    args = list(xs)
    if norm:
        in_specs += [pl.BlockSpec((1, kw), lambda i, j: (0, 0)) for kw in seg_widths]
        args += [g.reshape(1, -1).astype(F32) for g in gains]
    in_specs.append(pl.BlockSpec((K, bn), lambda i, j: (0, j)))
    args.append(w)
    if res is not None:
        in_specs.append(pl.BlockSpec((bm, bn), lambda i, j: (i, j)))
        args.append(res)
    if n_rope_blocks:
        cos, sin = rope
        pos_blocks = cos.shape[0] // bm
        in_specs += [pl.BlockSpec((bm, HEAD_DIM), lambda i, j: (i % pos_blocks, 0))] * 2
        args += [cos, sin]
    out_shape = [jax.ShapeDtypeStruct((T, N), out_dtype)]
    out_specs = [pl.BlockSpec((bm, bn), lambda i, j: (i, j))]
    if emit_xn:
        out_shape.append(jax.ShapeDtypeStruct((T, K), BF16))
        out_specs.append(pl.BlockSpec((bm, K), lambda i, j: (i, 0)))
    kern = functools.partial(_rms_matmul_kernel, seg_widths=seg_widths, norm=norm, has_res=res is not None,
                             n_rope_blocks=n_rope_blocks, emit_xn=emit_xn)
    outs = pl.pallas_call(
        kern,
        grid=(T // bm, N // bn),
        in_specs=in_specs,
        out_specs=out_specs,
        out_shape=out_shape,
        scratch_shapes=[pltpu.VMEM((bm, K), BF16)],
        compiler_params=_params(("arbitrary", "arbitrary")),
        name=name,
    )(*args)
    return outs if emit_xn else outs[0]


def _swa_kernel(q_ref, k_ref, v_ref, sink_ref, o_ref, *, seq):
    n = pl.program_id(2)
    band = 3 * SWA_BLOCK
    start = pl.multiple_of(jnp.clip((n - 1) * SWA_BLOCK, 0, seq - band), SWA_BLOCK)
    k = k_ref[pl.ds(start, band), :]
    v = v_ref[pl.ds(start, band), :]
    qpos = n * SWA_BLOCK + lax.broadcasted_iota(jnp.int32, (SWA_BLOCK, band), 0)
    kpos = start + lax.broadcasted_iota(jnp.int32, (SWA_BLOCK, band), 1)
    valid = jnp.abs(qpos - kpos) <= SWA_WINDOW
    scale = HEAD_DIM ** -0.5
    for g in range(SWA_GROUP):
        q = q_ref[:, g * HEAD_DIM:(g + 1) * HEAD_DIM]
        s = lax.dot_general(q, k, (((1,), (1,)), ((), ())), preferred_element_type=F32) * scale
        s = jnp.where(valid, s, NEG_INF)
        sink = sink_ref[0, g:g + 1, 0:1]
        m = jnp.maximum(jnp.max(s, axis=-1, keepdims=True), sink)
        p = jnp.exp(s - m)
        denom = jnp.sum(p, axis=-1, keepdims=True) + jnp.exp(sink - m)
        p = (p / denom).astype(BF16)
        o_ref[:, g * HEAD_DIM:(g + 1) * HEAD_DIM] = jnp.dot(p, v, preferred_element_type=F32)


def _swa_attention(qkv, sink, batch, seq):
    T = batch * seq
    nb = seq // SWA_BLOCK
    gw = SWA_GROUP * HEAD_DIM
    k_col0 = SWA_Q_W // HEAD_DIM
    v_col0 = (SWA_Q_W + SWA_KV_W) // HEAD_DIM
    sink_b = jnp.broadcast_to(sink.astype(F32).reshape(SWA_KV_HEADS, SWA_GROUP, 1),
                              (SWA_KV_HEADS, SWA_GROUP, LANES))
    return pl.pallas_call(
        functools.partial(_swa_kernel, seq=seq),
        grid=(batch, SWA_KV_HEADS, nb),
        in_specs=[
            pl.BlockSpec((SWA_BLOCK, gw), lambda b, h, n: (b * nb + n, h)),
            pl.BlockSpec((seq, HEAD_DIM), lambda b, h, n: (b, k_col0 + h)),
            pl.BlockSpec((seq, HEAD_DIM), lambda b, h, n: (b, v_col0 + h)),
            pl.BlockSpec((1, SWA_GROUP, LANES), lambda b, h, n: (h, 0, 0)),
        ],
        out_specs=pl.BlockSpec((SWA_BLOCK, gw), lambda b, h, n: (b * nb + n, h)),
        out_shape=jax.ShapeDtypeStruct((T, SWA_Q_W), F32),
        compiler_params=_params(("arbitrary", "arbitrary", "arbitrary")),
        name="swa_attention",
    )(qkv, qkv, qkv, sink_b)


def _nat_tables(seq):
    rows = seq // GRID_W
    kr = min(NAT_KR_MAX, rows)
    span = min(rows, NAT_QROWS + kr - 1)
    nb = rows // NAT_QROWS
    qbl = NAT_QROWS * GRID_W
    r0 = np.arange(nb) * NAT_QROWS
    kstart = np.clip(r0 - kr // 2, 0, rows - span)
    key_rows = kstart[:, None] + np.arange(span)[None, :]
    q_row = r0[:, None] + (np.arange(qbl) // GRID_W)[None, :]
    q_col = np.broadcast_to((np.arange(qbl) % GRID_W)[None, :], q_row.shape)
    k_row = np.repeat(key_rows, GRID_W, axis=1)
    k_col = np.broadcast_to(np.tile(np.arange(GRID_W), span)[None, :], k_row.shape)
    rs = np.clip(q_row - kr // 2, 0, rows - kr)[:, :, None]
    cs = np.clip(q_col - NAT_KC // 2, 0, GRID_W - NAT_KC)[:, :, None]
    kr3, kc3 = k_row[:, None, :], k_col[:, None, :]
    valid = (kr3 >= rs) & (kr3 < rs + kr) & (kc3 >= cs) & (kc3 < cs + NAT_KC)
    dr = np.clip(kr3 - q_row[:, :, None] + NAT_KR_MAX - 1, 0, 2 * NAT_KR_MAX - 2)
    dc = np.clip(kc3 - q_col[:, :, None] + NAT_KC - 1, 0, 2 * NAT_KC - 2)
    flat = np.where(valid, dr * (2 * NAT_KC - 1) + dc, -1).reshape(nb, -1)
    _, first, inverse = np.unique(flat, axis=0, return_index=True, return_inverse=True)
    return dict(nb=nb, qbl=qbl, nkeys=span * GRID_W, start=(kstart * GRID_W).astype(np.int32),
                pid=np.asarray(inverse).reshape(-1).astype(np.int32),
                valid=valid[first], dr=dr[first], dc=dc[first])


def _nat_kernel(pid_ref, start_ref, q_ref, k_ref, v_ref, bias_ref, o_ref, *, nkeys, heads):
    del pid_ref
    n = pl.program_id(2)
    start = pl.multiple_of(start_ref[n], GRID_W)
    scale = HEAD_DIM ** -0.5
    for h in range(heads):
        sl = slice(h * HEAD_DIM, (h + 1) * HEAD_DIM)
        q = q_ref[:, sl]
        k = k_ref[pl.ds(start, nkeys), sl]
        v = v_ref[pl.ds(start, nkeys), sl]
        s = lax.dot_general(q, k, (((1,), (1,)), ((), ())), preferred_element_type=F32) * scale
        s = s + bias_ref[0, h]
        m = jnp.max(s, axis=-1, keepdims=True)
        p = jnp.exp(s - m)
        p = (p / jnp.sum(p, axis=-1, keepdims=True)).astype(BF16)
        o_ref[:, sl] = jnp.dot(p, v, preferred_element_type=F32)


def _nat_attention(qkv, rel_bias, batch, seq):
    T = batch * seq
    tab = _nat_tables(seq)
    nb, qbl, nkeys = tab["nb"], tab["qbl"], tab["nkeys"]
    bias = jnp.where(tab["valid"][None], rel_bias.astype(F32)[:, tab["dr"], tab["dc"]], NEG_INF)
    bias = jnp.transpose(bias, (1, 0, 2, 3))
    hpg = 4
    gw = hpg * HEAD_DIM
    n_hg = NAT_HEADS // hpg
    q0 = (SWA_Q_W + 2 * SWA_KV_W) // gw
    k0 = q0 + n_hg
    v0 = k0 + n_hg
    grid_spec = pltpu.PrefetchScalarGridSpec(
        num_scalar_prefetch=2,
        grid=(batch, n_hg, nb),
        in_specs=[
            pl.BlockSpec((qbl, gw), lambda b, g, n, pid, st: (b * nb + n, q0 + g)),
            pl.BlockSpec((seq, gw), lambda b, g, n, pid, st: (b, k0 + g)),
            pl.BlockSpec((seq, gw), lambda b, g, n, pid, st: (b, v0 + g)),
            pl.BlockSpec((1, hpg, qbl, nkeys), lambda b, g, n, pid, st: (pid[n], g, 0, 0)),
        ],
        out_specs=pl.BlockSpec((qbl, gw), lambda b, g, n, pid, st: (b * nb + n, g)),
    )
    return pl.pallas_call(
        functools.partial(_nat_kernel, nkeys=nkeys, heads=hpg),
        grid_spec=grid_spec,
        out_shape=jax.ShapeDtypeStruct((T, NAT_W), F32),
        compiler_params=_params(("arbitrary", "arbitrary", "arbitrary")),
        name="nat_attention",
    )(jnp.asarray(tab["pid"]), jnp.asarray(tab["start"]), qkv, qkv, qkv, bias)


def _mem_attn_kernel(q_ref, kv_ref, o_ref):
    scale = MEM_HEAD_DIM ** -0.5
    for h in range(MEM_HEADS):
        sl = slice(h * MEM_HEAD_DIM, (h + 1) * MEM_HEAD_DIM)
        q = q_ref[:, sl]
        k = kv_ref[:, sl]
        v = kv_ref[:, MEM_W + h * MEM_HEAD_DIM:MEM_W + (h + 1) * MEM_HEAD_DIM]
        s = lax.dot_general(q, k, (((1,), (1,)), ((), ())), preferred_element_type=F32) * scale
        m = jnp.max(s, axis=-1, keepdims=True)
        p = jnp.exp(s - m)
        p = (p / jnp.sum(p, axis=-1, keepdims=True)).astype(BF16)
        o_ref[:, sl] = jnp.dot(p, v, preferred_element_type=F32).astype(o_ref.dtype)


def _mem_attention(q, kv, batch, seq, mem_len, *, bq=512):
    T = batch * seq
    nq = seq // bq
    return pl.pallas_call(
        _mem_attn_kernel,
        grid=(batch, nq),
        in_specs=[
            pl.BlockSpec((bq, MEM_W), lambda b, i: (b * nq + i, 0)),
            pl.BlockSpec((mem_len, 2 * MEM_W), lambda b, i: (b, 0)),
        ],
        out_specs=pl.BlockSpec((bq, MEM_W), lambda b, i: (b * nq + i, 0)),
        out_shape=jax.ShapeDtypeStruct((T, MEM_W), BF16),
        compiler_params=_params(("arbitrary", "arbitrary")),
        name="mem_attention",
    )(q, kv)


def _top16_rows(s, v_scr):
    rows = lax.broadcasted_iota(jnp.int32, s.shape, 0).astype(F32)
    sentinel = float(s.shape[0])

    def body(a, carry):
        work, rank = carry
        m = jnp.max(work, axis=0, keepdims=True)
        first = jnp.min(jnp.where(work == m, rows, sentinel), axis=0, keepdims=True)
        hit = rows == first
        v_scr[pl.ds(a, 1), :] = m
        rank = jnp.where(hit, a.astype(F32), rank)
        work = jnp.where(hit, -jnp.inf, work)
        return work, rank

    _, rank = lax.fori_loop(0, PEER_TOPK, body, (s, jnp.full(s.shape, float(PEER_TOPK), F32)))
    return rank


def _route_kernel(q_ref, keys_ref, n1_ref, e1_ref, r2_ref, e2_ref, v1_scr, v2_scr, c_scr, sel_scr, *, tb):
    half = PEER_QDIM // 2
    dn = (((1,), (1,)), ((), ()))
    s1_all = lax.dot_general(keys_ref[0, 0], q_ref[:, :half], dn, preferred_element_type=F32)
    s2_all = lax.dot_general(keys_ref[0, 1], q_ref[:, half:], dn, preferred_element_type=F32)
    for c in range(tb // LANES):
        cols = slice(c * LANES, (c + 1) * LANES)
        s1 = s1_all[:, cols]
        s2 = s2_all[:, cols]
        rank1 = _top16_rows(s1, v1_scr)
        rank2 = _top16_rows(s2, v2_scr)
        for p, (a, b) in enumerate(_CELLS):
            c_scr[p:p + 1, :] = v1_scr[a:a + 1, :] + v2_scr[b:b + 1, :]
        if _CELL_ROWS > _N_CELLS:
            c_scr[_N_CELLS:, :] = jnp.full((_CELL_ROWS - _N_CELLS, LANES), -jnp.inf, F32)
        cand = c_scr[...]
        prow = lax.broadcasted_iota(jnp.int32, cand.shape, 0).astype(F32)

        def body(t, carry):
            work, picked = carry
            m = jnp.max(work, axis=0, keepdims=True)
            first = jnp.min(jnp.where(work == m, prow, float(_CELL_ROWS)), axis=0, keepdims=True)
            hit = prow == first
            return jnp.where(hit, -jnp.inf, work), jnp.where(hit, 1.0, picked)

        _, picked = lax.fori_loop(0, PEER_TOPK, body, (cand, jnp.zeros(cand.shape, F32)))
        top1 = v1_scr[0:1, :]
        top2 = v2_scr[0:1, :]
        z = jnp.sum(jnp.where(picked > 0, jnp.exp(cand - (top1 + top2)), 0.0), axis=0, keepdims=True)
        sel_scr[...] = picked
        n1 = jnp.zeros(s1.shape, F32)
        p0 = 0
        for a in range(PEER_TOPK):
            width = sum(1 for (aa, _) in _CELLS if aa == a)
            n_a = jnp.sum(sel_scr[p0:p0 + width, :], axis=0, keepdims=True)
            n1 = n1 + jnp.where(rank1 == float(a), n_a, 0.0)
            p0 += width
        n1_ref[0, :, cols] = n1
        e1_ref[0, :, cols] = jnp.exp(s1 - top1) / z
        r2_ref[0, :, cols] = rank2
        e2_ref[0, :, cols] = jnp.exp(s2 - top2)


def _peer_route(q, sub_keys, *, tb=512):
    T = q.shape[0]
    shape = jax.ShapeDtypeStruct((PEER_HEADS, PEER_NKEYS, T), F32)
    out_spec = pl.BlockSpec((1, PEER_NKEYS, tb), lambda i, h: (h, 0, i))
    return pl.pallas_call(
        functools.partial(_route_kernel, tb=tb),
        grid=(T // tb, PEER_HEADS),
        in_specs=[
            pl.BlockSpec((tb, PEER_QDIM), lambda i, h: (i, h)),
            pl.BlockSpec((1, 2, PEER_NKEYS, PEER_QDIM // 2), lambda i, h: (h, 0, 0, 0)),
        ],
        out_specs=[out_spec] * 4,
        out_shape=[shape] * 4,
        scratch_shapes=[pltpu.VMEM((PEER_TOPK, LANES), F32), pltpu.VMEM((PEER_TOPK, LANES), F32),
                        pltpu.VMEM((_CELL_ROWS, LANES), F32), pltpu.VMEM((_CELL_ROWS, LANES), F32)],
        compiler_params=_params(("arbitrary", "arbitrary")),
        name="peer_route",
    )(q, sub_keys)


def _peer_kernel(xn_ref, u_ref, v_ref, n1_ref, e1_ref, r2_ref, e2_ref, o_ref, *, eb):
    e = pl.program_id(1)

    @pl.when(e == 0)
    def _():
        o_ref[...] = jnp.zeros(o_ref.shape, F32)

    a_t = lax.dot_general(u_ref[...], xn_ref[...], (((1,), (1,)), ((), ())), preferred_element_type=F32)
    act = 0.5 * a_t * (1.0 + lax.erf(a_t * np.float32(np.sqrt(0.5))))
    n_i = eb // PEER_NKEYS
    parts = []
    for a in range(n_i):
        i_row = e * n_i + a
        gate = jnp.zeros((PEER_NKEYS, a_t.shape[1]), F32)
        for h in range(PEER_HEADS):
            cnt = n1_ref[h, pl.ds(i_row, 1), :]
            g1 = e1_ref[h, pl.ds(i_row, 1), :]
            gate = gate + jnp.where(r2_ref[h] < cnt, e2_ref[h] * g1, 0.0)
        parts.append((act[a * PEER_NKEYS:(a + 1) * PEER_NKEYS, :] * gate).astype(BF16))
    w_t = jnp.concatenate(parts, axis=0) if n_i > 1 else parts[0]
    o_ref[...] += lax.dot_general(w_t, v_ref[...], (((0,), (0,)), ((), ())), preferred_element_type=F32)


def _peer_mix(xn, u, v, n1, e1, r2, e2, *, tb=512, eb=512):
    T, D = xn.shape
    E = u.shape[0]
    once = pl.Buffered(1)
    route_spec = pl.BlockSpec((PEER_HEADS, PEER_NKEYS, tb), lambda i, e: (0, 0, i), pipeline_mode=once)
    return pl.pallas_call(
        functools.partial(_peer_kernel, eb=eb),
        grid=(T // tb, E // eb),
        in_specs=[
            pl.BlockSpec((tb, D), lambda i, e: (i, 0), pipeline_mode=once),
            pl.BlockSpec((eb, D), lambda i, e: (e, 0)),
            pl.BlockSpec((eb, D), lambda i, e: (e, 0)),
            route_spec, route_spec, route_spec, route_spec,
        ],
        out_specs=pl.BlockSpec((tb, D), lambda i, e: (i, 0)),
        out_shape=jax.ShapeDtypeStruct((T, D), F32),
        compiler_params=_params(("arbitrary", "arbitrary")),
        name="peer_mix",
    )(xn, u, v, n1, e1, r2, e2)


def _add_norm_kernel(a_ref, b_ref, g_ref, o_ref, *, norm):
    x = a_ref[...] + b_ref[...]
    if norm:
        ms = jnp.mean(x * x, axis=-1, keepdims=True)
        x = (x * lax.rsqrt(ms + RMS_EPS)) * g_ref[...]
    o_ref[...] = x


def _add_norm(a, b, gain, *, norm, bm=256):
    T, D = a.shape
    spec = pl.BlockSpec((bm, D), lambda i: (i, 0))
    return pl.pallas_call(
        functools.partial(_add_norm_kernel, norm=norm),
        grid=(T // bm,),
        in_specs=[spec, spec, pl.BlockSpec((1, D), lambda i: (0, 0))],
        out_specs=spec,
        out_shape=jax.ShapeDtypeStruct((T, D), F32),
        compiler_params=_params(("arbitrary",)),
        name="add_norm",
    )(a, b, gain.reshape(1, D).astype(F32))


def _rope_tables(seq):
    half = HEAD_DIM // 2
    inv = ROPE_THETA ** (-jnp.arange(half, dtype=F32) / half)
    ang = jnp.arange(seq, dtype=jnp.int32).astype(F32)[:, None] * inv[None, :]
    cos, sin = jnp.cos(ang), jnp.sin(ang)
    return jnp.concatenate([cos, cos], axis=-1), jnp.concatenate([-sin, sin], axis=-1)


def kernel(x, mem, norm_mix, w_in, swa_sink, nat_rel_bias, out_norm_swa, out_norm_nat, w_out, norm_mem_q,
           norm_mem_kv, w_mem_q, w_mem_k, w_mem_v, w_mem_o, norm_ffn, w_peer_q, peer_sub_keys, peer_u, peer_v,
           norm_final):
    B, S, D = x.shape
    M = mem.shape[1]
    T = B * S
    depth = w_in.shape[0]
    assert depth >= 1
    rope = _rope_tables(S)
    h = x.reshape(T, D)
    mem2 = mem.reshape(B * M, D)
    rope_cols = SWA_Q_W + SWA_KV_W
    bn_in = 512
    assert rope_cols % bn_in == 0
    for l in range(depth):
        qkv = _rms_matmul([h], [norm_mix[l]], w_in[l].astype(BF16), bm=512, bn=bn_in, out_dtype=BF16,
                          rope=rope, n_rope_blocks=rope_cols // bn_in, name="in_proj")
        oa = _swa_attention(qkv, swa_sink[l], B, S)
        ob = _nat_attention(qkv, nat_rel_bias[l], B, S)
        h = _rms_matmul([oa, ob], [out_norm_swa[l], out_norm_nat[l]], w_out[l].astype(BF16), bm=512, bn=512,
                        out_dtype=F32, res=h, name="out_proj")
        w_kv = jnp.concatenate([w_mem_k[l], w_mem_v[l]], axis=1).astype(BF16)
        kv = _rms_matmul([mem2], [norm_mem_kv[l]], w_kv, bm=512, bn=512, out_dtype=BF16, name="mem_kv_proj")
        qm = _rms_matmul([h], [norm_mem_q[l]], w_mem_q[l].astype(BF16), bm=512, bn=512, out_dtype=BF16,
                         name="mem_q_proj")
        om = _mem_attention(qm, kv, B, S, M)
        h = _rms_matmul([om], None, w_mem_o[l].astype(BF16), bm=512, bn=512, out_dtype=F32, res=h,
                        name="mem_o_proj")
        qp, xn = _rms_matmul([h], [norm_ffn[l]], w_peer_q[l].astype(BF16), bm=512, bn=512, out_dtype=BF16,
                             emit_xn=True, name="peer_q_proj")
        n1, e1, r2, e2 = _peer_route(qp, peer_sub_keys[l].astype(BF16))
        peer = _peer_mix(xn, peer_u[l].astype(BF16), peer_v[l].astype(BF16), n1, e1, r2, e2)
        last = l == depth - 1
        h = _add_norm(h, peer, norm_final if last else jnp.ones((D,), F32), norm=last)
    return h.reshape(B, S, D)
```

```python
import functools

import numpy as np
import jax
import jax.numpy as jnp
from jax import lax
from jax.experimental import pallas as pl
from jax.experimental.pallas import tpu as pltpu

F32 = jnp.float32
BF16 = jnp.bfloat16

D_MODEL = 4096
HEAD_DIM = 128
SWA_Q_HEADS = 16
SWA_KV_HEADS = 4
SWA_GROUP = SWA_Q_HEADS // SWA_KV_HEADS
SWA_WINDOW = 128
SWA_BLOCK = 128
ROPE_THETA = 10000.0
NAT_HEADS = 16
NAT_KR_MAX = 8
NAT_KC = 16
GRID_W = 64
NAT_QROWS = 2
SWA_Q_W = SWA_Q_HEADS * HEAD_DIM
SWA_KV_W = SWA_KV_HEADS * HEAD_DIM
NAT_W = NAT_HEADS * HEAD_DIM
MEM_HEADS = 4
MEM_HEAD_DIM = 256
MEM_W = MEM_HEADS * MEM_HEAD_DIM
PEER_HEADS = 8
PEER_NKEYS = 128
PEER_QDIM = 256
PEER_TOPK = 16
RMS_EPS = 1e-6
NEG_INF = -1e30

LANES = 128
VMEM_LIMIT_BYTES = 56 * 1024 * 1024

_CELLS = [(a, b) for a in range(PEER_TOPK) for b in range(PEER_TOPK) if (a + 1) * (b + 1) <= PEER_TOPK]
_N_CELLS = len(_CELLS)
_CELL_ROWS = -(-_N_CELLS // 8) * 8


def _params(sem):
    return pltpu.CompilerParams(dimension_semantics=sem, vmem_limit_bytes=VMEM_LIMIT_BYTES)


def _rms_matmul_kernel(*refs, seg_widths, norm, has_res, rope_cols, emit_xn):
    n_seg = len(seg_widths)
    pos = 0
    x_refs = refs[pos:pos + n_seg]; pos += n_seg
    g_refs = ()
    if norm:
        g_refs = refs[pos:pos + n_seg]; pos += n_seg
    w_ref = refs[pos]; pos += 1
    res_ref = None
    if has_res:
        res_ref = refs[pos]; pos += 1
    cos_ref = sin_ref = None
    if rope_cols:
        cos_ref, sin_ref = refs[pos], refs[pos + 1]; pos += 2
    o_ref = refs[pos]; pos += 1
    xn_out_ref = None
    if emit_xn:
        xn_out_ref = refs[pos]; pos += 1
    xn_ref = refs[pos]

    j = pl.program_id(1)

    @pl.when(j == 0)
    def _():
        off = 0
        for s in range(n_seg):
            x = x_refs[s][...].astype(F32)
            if norm:
                ms = jnp.mean(x * x, axis=-1, keepdims=True)
                x = (x * lax.rsqrt(ms + RMS_EPS)) * g_refs[s][...]
            xn_ref[:, off:off + seg_widths[s]] = x.astype(BF16)
            off += seg_widths[s]
        if emit_xn:
            xn_out_ref[...] = xn_ref[...]

    acc = jnp.dot(xn_ref[...], w_ref[...], preferred_element_type=F32)
    if has_res:
        acc = acc + res_ref[...]

    bn = acc.shape[1]
    if rope_cols:
        def store(n_roped_heads):
            for g in range(n_roped_heads):
                sl = slice(g * HEAD_DIM, (g + 1) * HEAD_DIM)
                rot = pltpu.roll(acc[:, sl], HEAD_DIM // 2, 1)
                o_ref[:, sl] = (acc[:, sl] * cos_ref[...] + rot * sin_ref[...]).astype(o_ref.dtype)
            if n_roped_heads * HEAD_DIM < bn:
                o_ref[:, n_roped_heads * HEAD_DIM:] = acc[:, n_roped_heads * HEAD_DIM:].astype(o_ref.dtype)

        full_blocks, rem = divmod(rope_cols, bn)
        pl.when(j < full_blocks)(lambda: store(bn // HEAD_DIM))
        if rem:
            pl.when(j == full_blocks)(lambda: store(rem // HEAD_DIM))
        pl.when(j >= full_blocks + (1 if rem else 0))(lambda: store(0))
    else:
        o_ref[...] = acc.astype(o_ref.dtype)


def _rms_matmul(xs, gains, w, *, bm, bn, out_dtype, res=None, rope=None, rope_cols=0, emit_xn=False, name):
    T = xs[0].shape[0]
    seg_widths = tuple(int(x.shape[1]) for x in xs)
    K = sum(seg_widths)
    N = w.shape[1]
    norm = gains is not None
    assert w.shape[0] == K and T % bm == 0 and N % bn == 0 and rope_cols % HEAD_DIM == 0
    in_specs = [pl.BlockSpec((bm, kw), lambda i, j: (i, 0)) for kw in seg_widths]
    args = list(xs)
    if norm:
        in_specs += [pl.BlockSpec((1, kw), lambda i, j: (0, 0)) for kw in seg_widths]
        args += [g.reshape(1, -1).astype(F32) for g in gains]
    in_specs.append(pl.BlockSpec((K, bn), lambda i, j: (0, j), pipeline_mode=pl.Buffered(1) if bn == N else None))
    args.append(w)
    if res is not None:
        in_specs.append(pl.BlockSpec((bm, bn), lambda i, j: (i, j)))
        args.append(res)
    if rope_cols:
        cos, sin = rope
        pos_blocks = cos.shape[0] // bm
        in_specs += [pl.BlockSpec((bm, HEAD_DIM), lambda i, j: (i % pos_blocks, 0))] * 2
        args += [cos, sin]
    out_shape = [jax.ShapeDtypeStruct((T, N), out_dtype)]
    out_specs = [pl.BlockSpec((bm, bn), lambda i, j: (i, j))]
    if emit_xn:
        out_shape.append(jax.ShapeDtypeStruct((T, K), BF16))
        out_specs.append(pl.BlockSpec((bm, K), lambda i, j: (i, 0)))
    kern = functools.partial(_rms_matmul_kernel, seg_widths=seg_widths, norm=norm, has_res=res is not None,
                             rope_cols=rope_cols, emit_xn=emit_xn)
    outs = pl.pallas_call(
        kern,
        grid=(T // bm, N // bn),
        in_specs=in_specs,
        out_specs=out_specs,
        out_shape=out_shape,
        scratch_shapes=[pltpu.VMEM((bm, K), BF16)],
        compiler_params=_params(("arbitrary", "arbitrary")),
        name=name,
    )(*args)
    return outs if emit_xn else outs[0]


def _swa_kernel(q_ref, k_ref, v_ref, sink_ref, o_ref, *, seq):
    n = pl.program_id(2)
    band = 3 * SWA_BLOCK
    start = pl.multiple_of(jnp.clip((n - 1) * SWA_BLOCK, 0, seq - band), SWA_BLOCK)
    k = k_ref[pl.ds(start, band), :]
    v = v_ref[pl.ds(start, band), :]
    qpos = n * SWA_BLOCK + lax.broadcasted_iota(jnp.int32, (SWA_BLOCK, band), 0)
    kpos = start + lax.broadcasted_iota(jnp.int32, (SWA_BLOCK, band), 1)
    valid = jnp.abs(qpos - kpos) <= SWA_WINDOW
    scale = HEAD_DIM ** -0.5
    q = jnp.concatenate([q_ref[:, g * HEAD_DIM:(g + 1) * HEAD_DIM] for g in range(SWA_GROUP)], axis=0)
    s = lax.dot_general(q, k, (((1,), (1,)), ((), ())), preferred_element_type=F32) * scale
    s = jnp.where(jnp.concatenate([valid] * SWA_GROUP, axis=0), s, NEG_INF)
    sink = jnp.concatenate([jnp.broadcast_to(sink_ref[0, g:g + 1, 0:1], (SWA_BLOCK, 1))
                            for g in range(SWA_GROUP)], axis=0)
    m = jnp.maximum(jnp.max(s, axis=-1, keepdims=True), sink)
    p = jnp.exp(s - m)
    denom = jnp.sum(p, axis=-1, keepdims=True) + jnp.exp(sink - m)
    o = jnp.dot((p / denom).astype(BF16), v, preferred_element_type=F32)
    for g in range(SWA_GROUP):
        o_ref[:, g * HEAD_DIM:(g + 1) * HEAD_DIM] = o[g * SWA_BLOCK:(g + 1) * SWA_BLOCK, :]


def _swa_attention(qkv, sink, batch, seq):
    T = batch * seq
    nb = seq // SWA_BLOCK
    gw = SWA_GROUP * HEAD_DIM
    k_col0 = SWA_Q_W // HEAD_DIM
    v_col0 = (SWA_Q_W + SWA_KV_W) // HEAD_DIM
    sink_b = jnp.broadcast_to(sink.astype(F32).reshape(SWA_KV_HEADS, SWA_GROUP, 1),
                              (SWA_KV_HEADS, SWA_GROUP, LANES))
    return pl.pallas_call(
        functools.partial(_swa_kernel, seq=seq),
        grid=(batch, SWA_KV_HEADS, nb),
        in_specs=[
            pl.BlockSpec((SWA_BLOCK, gw), lambda b, h, n: (b * nb + n, h)),
            pl.BlockSpec((seq, HEAD_DIM), lambda b, h, n: (b, k_col0 + h)),
            pl.BlockSpec((seq, HEAD_DIM), lambda b, h, n: (b, v_col0 + h)),
            pl.BlockSpec((1, SWA_GROUP, LANES), lambda b, h, n: (h, 0, 0)),
        ],
        out_specs=pl.BlockSpec((SWA_BLOCK, gw), lambda b, h, n: (b * nb + n, h)),
        out_shape=jax.ShapeDtypeStruct((T, SWA_Q_W), F32),
        compiler_params=_params(("arbitrary", "arbitrary", "arbitrary")),
        name="swa_attention",
    )(qkv, qkv, qkv, sink_b)


def _nat_tables(seq):
    rows = seq // GRID_W
    kr = min(NAT_KR_MAX, rows)
    span = min(rows, NAT_QROWS + kr - 1)
    nb = rows // NAT_QROWS
    qbl = NAT_QROWS * GRID_W
    r0 = np.arange(nb) * NAT_QROWS
    kstart = np.clip(r0 - kr // 2, 0, rows - span)
    key_rows = kstart[:, None] + np.arange(span)[None, :]
    q_row = r0[:, None] + (np.arange(qbl) // GRID_W)[None, :]
    q_col = np.broadcast_to((np.arange(qbl) % GRID_W)[None, :], q_row.shape)
    k_row = np.repeat(key_rows, GRID_W, axis=1)
    k_col = np.broadcast_to(np.tile(np.arange(GRID_W), span)[None, :], k_row.shape)
    rs = np.clip(q_row - kr // 2, 0, rows - kr)[:, :, None]
    cs = np.clip(q_col - NAT_KC // 2, 0, GRID_W - NAT_KC)[:, :, None]
    kr3, kc3 = k_row[:, None, :], k_col[:, None, :]
    valid = (kr3 >= rs) & (kr3 < rs + kr) & (kc3 >= cs) & (kc3 < cs + NAT_KC)
    dr = np.clip(kr3 - q_row[:, :, None] + NAT_KR_MAX - 1, 0, 2 * NAT_KR_MAX - 2)
    dc = np.clip(kc3 - q_col[:, :, None] + NAT_KC - 1, 0, 2 * NAT_KC - 2)
    flat = np.where(valid, dr * (2 * NAT_KC - 1) + dc, -1).reshape(nb, -1)
    _, first, inverse = np.unique(flat, axis=0, return_index=True, return_inverse=True)
    valid, dr, dc = valid[first], dr[first], dc[first]
    dr_tab = dr[:, ::GRID_W, ::GRID_W]
    assert (dr == np.repeat(np.repeat(dr_tab, GRID_W, axis=1), GRID_W, axis=2)).all()
    col = np.arange(GRID_W)
    toeplitz = np.clip(col[None, :] - col[:, None] + NAT_KC - 1, 0, 2 * NAT_KC - 2)
    assert (dc == np.tile(toeplitz, (NAT_QROWS, span))[None]).all()
    far = np.tile(np.abs(col[None, :] - col[:, None]) >= NAT_KC, (NAT_QROWS, span))
    assert not (valid & far[None]).any()
    return dict(nb=nb, qbl=qbl, span=span, nkeys=span * GRID_W, start=(kstart * GRID_W).astype(np.int32),
                pid=np.asarray(inverse).reshape(-1).astype(np.int32), valid=valid, dr_tab=dr_tab)


def _nat_bias_kernel(rows_ref, valid_ref, o_ref, *, span):
    n_pairs = -(-span // 2)
    for qr in range(NAT_QROWS):
        rs = slice(qr * GRID_W, (qr + 1) * GRID_W)
        for pair in range(n_pairs):
            row = rows_ref[0, 0, qr * n_pairs + pair:qr * n_pairs + pair + 1, :]
            blk = pltpu.roll(jnp.broadcast_to(row, (GRID_W, LANES)), LANES - (NAT_KC - 1), 1,
                             stride=1, stride_axis=0)
            width = min(LANES, span * GRID_W - pair * LANES)
            cs = slice(pair * LANES, pair * LANES + width)
            o_ref[0, 0, rs, cs] = jnp.where(valid_ref[0, rs, cs] > 0, blk[:, :width], NEG_INF)


def _nat_bias_table(rel_bias, tab):
    span, qbl, nkeys = tab["span"], tab["qbl"], tab["nkeys"]
    n_pat = tab["valid"].shape[0]
    n_pairs = -(-span // 2)
    nrel = 2 * NAT_KC - 1
    assert 2 * GRID_W == LANES and nrel <= GRID_W
    rows = rel_bias.astype(F32)[:, tab["dr_tab"]]
    rows = jnp.pad(rows, ((0, 0), (0, 0), (0, 0), (0, 2 * n_pairs - span), (0, GRID_W - nrel)))
    rows = jnp.transpose(rows.reshape(NAT_HEADS, n_pat, NAT_QROWS * n_pairs, LANES), (1, 0, 2, 3))
    valid = jnp.asarray(tab["valid"].astype(np.float32))
    return pl.pallas_call(
        functools.partial(_nat_bias_kernel, span=span),
        grid=(n_pat, NAT_HEADS),
        in_specs=[
            pl.BlockSpec((1, 1, NAT_QROWS * n_pairs, LANES), lambda p, h: (p, h, 0, 0)),
            pl.BlockSpec((1, qbl, nkeys), lambda p, h: (p, 0, 0)),
        ],
        out_specs=pl.BlockSpec((1, 1, qbl, nkeys), lambda p, h: (p, h, 0, 0)),
        out_shape=jax.ShapeDtypeStruct((n_pat, NAT_HEADS, qbl, nkeys), F32),
        compiler_params=_params(("arbitrary", "arbitrary")),
        name="nat_bias_table",
    )(rows, valid)


def _nat_kernel(pid_ref, start_ref, q_ref, k_ref, v_ref, bias_ref, o_ref, *, nkeys, heads):
    del pid_ref
    n = pl.program_id(2)
    start = pl.multiple_of(start_ref[n], GRID_W)
    scale = HEAD_DIM ** -0.5
    sls = [slice(h * HEAD_DIM, (h + 1) * HEAD_DIM) for h in range(heads)]
    q = jnp.stack([q_ref[:, sl] for sl in sls], axis=0)
    k = jnp.stack([k_ref[pl.ds(start, nkeys), sl] for sl in sls], axis=0)
    v = jnp.stack([v_ref[pl.ds(start, nkeys), sl] for sl in sls], axis=0)
    s = lax.dot_general(q, k, (((2,), (2,)), ((0,), (0,))), preferred_element_type=F32) * scale
    s = s + bias_ref[0]
    m = jnp.max(s, axis=-1, keepdims=True)
    p = jnp.exp(s - m)
    p = (p / jnp.sum(p, axis=-1, keepdims=True)).astype(BF16)
    o = lax.dot_general(p, v, (((2,), (1,)), ((0,), (0,))), preferred_element_type=F32)
    for h, sl in enumerate(sls):
        o_ref[:, sl] = o[h]


def _nat_attention(qkv, rel_bias, batch, seq):
    T = batch * seq
    tab = _nat_tables(seq)
    nb, qbl, nkeys = tab["nb"], tab["qbl"], tab["nkeys"]
    bias = _nat_bias_table(rel_bias, tab)
    hpg = 4
    gw = hpg * HEAD_DIM
    n_hg = NAT_HEADS // hpg
    q0 = (SWA_Q_W + 2 * SWA_KV_W) // gw
    k0 = q0 + n_hg
    v0 = k0 + n_hg
    grid_spec = pltpu.PrefetchScalarGridSpec(
        num_scalar_prefetch=2,
        grid=(batch, n_hg, nb),
        in_specs=[
            pl.BlockSpec((qbl, gw), lambda b, g, n, pid, st: (b * nb + n, q0 + g)),
            pl.BlockSpec((seq, gw), lambda b, g, n, pid, st: (b, k0 + g)),
            pl.BlockSpec((seq, gw), lambda b, g, n, pid, st: (b, v0 + g)),
            pl.BlockSpec((1, hpg, qbl, nkeys), lambda b, g, n, pid, st: (pid[n], g, 0, 0)),
        ],
        out_specs=pl.BlockSpec((qbl, gw), lambda b, g, n, pid, st: (b * nb + n, g)),
    )
    return pl.pallas_call(
        functools.partial(_nat_kernel, nkeys=nkeys, heads=hpg),
        grid_spec=grid_spec,
        out_shape=jax.ShapeDtypeStruct((T, NAT_W), F32),
        compiler_params=_params(("arbitrary", "arbitrary", "arbitrary")),
        name="nat_attention",
    )(jnp.asarray(tab["pid"]), jnp.asarray(tab["start"]), qkv, qkv, qkv, bias)


def _mem_attn_kernel(q_ref, kv_ref, o_ref):
    scale = MEM_HEAD_DIM ** -0.5
    for h in range(MEM_HEADS):
        sl = slice(h * MEM_HEAD_DIM, (h + 1) * MEM_HEAD_DIM)
        q = q_ref[:, sl]
        k = kv_ref[:, sl]
        v = kv_ref[:, MEM_W + h * MEM_HEAD_DIM:MEM_W + (h + 1) * MEM_HEAD_DIM]
        s = lax.dot_general(q, k, (((1,), (1,)), ((), ())), preferred_element_type=F32) * scale
        m = jnp.max(s, axis=-1, keepdims=True)
        p = jnp.exp(s - m)
        p = (p / jnp.sum(p, axis=-1, keepdims=True)).astype(BF16)
        o_ref[:, sl] = jnp.dot(p, v, preferred_element_type=F32).astype(o_ref.dtype)


def _mem_attention(q, kv, batch, seq, mem_len, *, bq=512):
    T = batch * seq
    nq = seq // bq
    return pl.pallas_call(
        _mem_attn_kernel,
        grid=(batch, nq),
        in_specs=[
            pl.BlockSpec((bq, MEM_W), lambda b, i: (b * nq + i, 0)),
            pl.BlockSpec((mem_len, 2 * MEM_W), lambda b, i: (b, 0)),
        ],
        out_specs=pl.BlockSpec((bq, MEM_W), lambda b, i: (b * nq + i, 0)),
        out_shape=jax.ShapeDtypeStruct((T, MEM_W), BF16),
        compiler_params=_params(("arbitrary", "arbitrary")),
        name="mem_attention",
    )(q, kv)


def _extract_max(work, rows):
    m = jnp.max(work, axis=0, keepdims=True)
    first = jnp.min(jnp.where(work == m, rows, float(work.shape[0])), axis=0, keepdims=True)
    return m, rows == first


def _top16_rows(scores, dests, work_scr, rank_scr, base):
    shape = scores[0].shape
    rows = lax.broadcasted_iota(jnp.int32, shape, 0).astype(F32)
    for n, s in enumerate(scores):
        work_scr[base + n] = s
        rank_scr[base + n] = jnp.full(shape, float(PEER_TOPK), F32)

    def body(a, carry):
        for n in range(len(scores)):
            work = work_scr[base + n]
            m, hit = _extract_max(work, rows)
            ref, idx = dests[n]
            ref[idx, pl.ds(a, 1), :] = m
            work_scr[base + n] = jnp.where(hit, -jnp.inf, work)
            rank_scr[base + n] = jnp.where(hit, a.astype(F32), rank_scr[base + n])
        return carry

    lax.fori_loop(0, PEER_TOPK, body, 0)
    return [rank_scr[base + n] for n in range(len(scores))]


def _route_kernel(q_ref, keys_ref, n1_ref, e1_ref, r2_ref, e2_ref, v1_scr, v2_scr, c_scr, sel_scr, work_scr, rank_scr,
                  *, tb, group):
    half = PEER_QDIM // 2
    dn = (((1,), (1,)), ((), ()))
    s1_all = lax.dot_general(keys_ref[0, 0], q_ref[:, :half], dn, preferred_element_type=F32)
    s2_all = lax.dot_general(keys_ref[0, 1], q_ref[:, half:], dn, preferred_element_type=F32)
    n_c = tb // LANES
    cols = [slice(c * LANES, (c + 1) * LANES) for c in range(n_c)]
    s1 = [s1_all[:, cl] for cl in cols]
    s2 = [s2_all[:, cl] for cl in cols]
    rank1, rank2 = [], []
    for c0 in range(0, n_c, group):
        cs = list(range(c0, c0 + group))
        ranks = _top16_rows([s1[c] for c in cs] + [s2[c] for c in cs],
                            [(v1_scr, c) for c in cs] + [(v2_scr, c) for c in cs], work_scr, rank_scr, 2 * c0)
        rank1 += ranks[:group]
        rank2 += ranks[group:]
    for c in range(n_c):
        for p, (a, b) in enumerate(_CELLS):
            c_scr[c, p:p + 1, :] = v1_scr[c, a:a + 1, :] + v2_scr[c, b:b + 1, :]
        if _CELL_ROWS > _N_CELLS:
            c_scr[c, _N_CELLS:, :] = jnp.full((_CELL_ROWS - _N_CELLS, LANES), -jnp.inf, F32)
    cand = [c_scr[c] for c in range(n_c)]
    prow = lax.broadcasted_iota(jnp.int32, cand[0].shape, 0).astype(F32)

    def body(t, carry):
        out = []
        for work, picked in carry:
            _, hit = _extract_max(work, prow)
            out.append((jnp.where(hit, -jnp.inf, work), jnp.where(hit, 1.0, picked)))
        return tuple(out)

    picked = [pk for _, pk in lax.fori_loop(0, PEER_TOPK, body,
                                            tuple((cd, jnp.zeros(cd.shape, F32)) for cd in cand))]
    for c in range(n_c):
        top1 = v1_scr[c, 0:1, :]
        top2 = v2_scr[c, 0:1, :]
        z = jnp.sum(jnp.where(picked[c] > 0, jnp.exp(cand[c] - (top1 + top2)), 0.0), axis=0, keepdims=True)
        sel_scr[c] = picked[c]
        n1 = jnp.zeros(s1[c].shape, F32)
        p0 = 0
        for a in range(PEER_TOPK):
            width = sum(1 for (aa, _) in _CELLS if aa == a)
            n_a = jnp.sum(sel_scr[c, p0:p0 + width, :], axis=0, keepdims=True)
            n1 = n1 + jnp.where(rank1[c] == float(a), n_a, 0.0)
            p0 += width
        n1_ref[0, :, cols[c]] = n1
        e1_ref[0, :, cols[c]] = jnp.exp(s1[c] - top1) / z
        r2_ref[0, :, cols[c]] = rank2[c]
        e2_ref[0, :, cols[c]] = jnp.exp(s2[c] - top2)


def _peer_route(q, sub_keys, *, tb=512, group=2):
    T = q.shape[0]
    shape = jax.ShapeDtypeStruct((PEER_HEADS, PEER_NKEYS, T), F32)
    out_spec = pl.BlockSpec((1, PEER_NKEYS, tb), lambda i, h: (h, 0, i))
    return pl.pallas_call(
        functools.partial(_route_kernel, tb=tb, group=group),
        grid=(T // tb, PEER_HEADS),
        in_specs=[
            pl.BlockSpec((tb, PEER_QDIM), lambda i, h: (i, h)),
            pl.BlockSpec((1, 2, PEER_NKEYS, PEER_QDIM // 2), lambda i, h: (h, 0, 0, 0)),
        ],
        out_specs=[out_spec] * 4,
        out_shape=[shape] * 4,
        scratch_shapes=[pltpu.VMEM((tb // LANES, PEER_TOPK, LANES), F32),
                        pltpu.VMEM((tb // LANES, PEER_TOPK, LANES), F32),
                        pltpu.VMEM((tb // LANES, _CELL_ROWS, LANES), F32),
                        pltpu.VMEM((tb // LANES, _CELL_ROWS, LANES), F32),
                        pltpu.VMEM((2 * (tb // LANES), PEER_NKEYS, LANES), F32),
                        pltpu.VMEM((2 * (tb // LANES), PEER_NKEYS, LANES), F32)],
        compiler_params=_params(("arbitrary", "arbitrary")),
        name="peer_route",
    )(q, sub_keys)


def _peer_kernel(xn_ref, u_ref, vt_ref, n1_ref, e1_ref, r2_ref, e2_ref, o_ref, a_scr, w_scr, *, eb, tc):
    k = pl.program_id(1)
    nt = (((1,), (1,)), ((), ()))

    @pl.when(k == 0)
    def _():
        o_ref[...] = jnp.zeros(o_ref.shape, F32)
        a_scr[...] = lax.dot_general(u_ref[...], xn_ref[...], nt, preferred_element_type=F32)

    @pl.when(k > 0)
    def _():
        n_i = eb // PEER_NKEYS
        for c in range(a_scr.shape[1] // tc):
            cols = slice(c * tc, (c + 1) * tc)
            for a in range(n_i):
                rows = slice(a * PEER_NKEYS, (a + 1) * PEER_NKEYS)
                i_row = k * n_i + (a - n_i)
                a_t = a_scr[rows, cols]
                act = 0.5 * a_t * (1.0 + lax.erf(a_t * np.float32(np.sqrt(0.5))))
                gate = jnp.zeros((PEER_NKEYS, tc), F32)
                for h in range(PEER_HEADS):
                    cnt = n1_ref[h, pl.ds(i_row, 1), :][:, cols]
                    g1 = e1_ref[h, pl.ds(i_row, 1), :][:, cols]
                    gate = gate + jnp.where(r2_ref[h, :, cols] < cnt, e2_ref[h, :, cols] * g1, 0.0)
                w_scr[rows, cols] = (act * gate).astype(BF16)
        a_scr[...] = lax.dot_general(u_ref[...], xn_ref[...], nt, preferred_element_type=F32)
        o_ref[...] += jnp.dot(vt_ref[...], w_scr[...], preferred_element_type=F32)


def _peer_mix(xn, u, vt, n1, e1, r2, e2, *, tb=512, eb=512, tc=128):
    T, D = xn.shape
    E = u.shape[0]
    n_e = E // eb
    once = pl.Buffered(1)
    route_spec = pl.BlockSpec((PEER_HEADS, PEER_NKEYS, tb), lambda i, k: (0, 0, i), pipeline_mode=once)
    return pl.pallas_call(
        functools.partial(_peer_kernel, eb=eb, tc=tc),
        grid=(T // tb, n_e + 1),
        in_specs=[
            pl.BlockSpec((tb, D), lambda i, k: (i, 0), pipeline_mode=once),
            pl.BlockSpec((eb, D), lambda i, k: (jnp.minimum(k, n_e - 1), 0)),
            pl.BlockSpec((D, eb), lambda i, k: (0, jnp.maximum(k - 1, 0))),
            route_spec, route_spec, route_spec, route_spec,
        ],
        out_specs=pl.BlockSpec((D, tb), lambda i, k: (0, i)),
        out_shape=jax.ShapeDtypeStruct((D, T), F32),
        scratch_shapes=[pltpu.VMEM((eb, tb), F32), pltpu.VMEM((eb, tb), BF16)],
        compiler_params=_params(("arbitrary", "arbitrary")),
        name="peer_mix",
    )(xn, u, vt, n1, e1, r2, e2)


def _add_norm_kernel(a_ref, bt_ref, g_ref, o_ref, *, norm):
    x = a_ref[...] + bt_ref[...].T
    if norm:
        ms = jnp.mean(x * x, axis=-1, keepdims=True)
        x = (x * lax.rsqrt(ms + RMS_EPS)) * g_ref[...]
    o_ref[...] = x


def _add_norm(a, bt, gain, *, norm, bm=256):
    T, D = a.shape
    spec = pl.BlockSpec((bm, D), lambda i: (i, 0))
    return pl.pallas_call(
        functools.partial(_add_norm_kernel, norm=norm),
        grid=(T // bm,),
        in_specs=[spec, pl.BlockSpec((D, bm), lambda i: (0, i)), pl.BlockSpec((1, D), lambda i: (0, 0))],
        out_specs=spec,
        out_shape=jax.ShapeDtypeStruct((T, D), F32),
        compiler_params=_params(("arbitrary",)),
        name="add_norm",
    )(a, bt, gain.reshape(1, D).astype(F32))


def _rope_tables(seq):
    half = HEAD_DIM // 2
    inv = ROPE_THETA ** (-jnp.arange(half, dtype=F32) / half)
    ang = jnp.arange(seq, dtype=jnp.int32).astype(F32)[:, None] * inv[None, :]
    cos, sin = jnp.cos(ang), jnp.sin(ang)
    return jnp.concatenate([cos, cos], axis=-1), jnp.concatenate([-sin, sin], axis=-1)


def kernel(x, mem, norm_mix, w_in, swa_sink, nat_rel_bias, out_norm_swa, out_norm_nat, w_out, norm_mem_q,
           norm_mem_kv, w_mem_q, w_mem_k, w_mem_v, w_mem_o, norm_ffn, w_peer_q, peer_sub_keys, peer_u, peer_v,
           norm_final):
    B, S, D = x.shape
    M = mem.shape[1]
    T = B * S
    depth = w_in.shape[0]
    assert depth >= 1
    rope = _rope_tables(S)
    h = x.reshape(T, D)
    mem2 = mem.reshape(B * M, D)
    rope_cols = SWA_Q_W + SWA_KV_W
    for l in range(depth):
        qkv = _rms_matmul([h], [norm_mix[l]], w_in[l].astype(BF16), bm=512, bn=1024, out_dtype=BF16,
                          rope=rope, rope_cols=rope_cols, name="in_proj")
        oa = _swa_attention(qkv, swa_sink[l], B, S)
        ob = _nat_attention(qkv, nat_rel_bias[l], B, S)
        h = _rms_matmul([oa, ob], [out_norm_swa[l], out_norm_nat[l]], w_out[l].astype(BF16), bm=512, bn=1024,
                        out_dtype=F32, res=h, name="out_proj")
        w_kv = jnp.concatenate([w_mem_k[l], w_mem_v[l]], axis=1).astype(BF16)
        kv = _rms_matmul([mem2], [norm_mem_kv[l]], w_kv, bm=512, bn=512, out_dtype=BF16, name="mem_kv_proj")
        qm = _rms_matmul([h], [norm_mem_q[l]], w_mem_q[l].astype(BF16), bm=512, bn=MEM_W, out_dtype=BF16,
                         name="mem_q_proj")
        om = _mem_attention(qm, kv, B, S, M)
        h = _rms_matmul([om], None, w_mem_o[l].astype(BF16), bm=512, bn=D, out_dtype=F32, res=h,
                        name="mem_o_proj")
        qp, xn = _rms_matmul([h], [norm_ffn[l]], w_peer_q[l].astype(BF16), bm=512, bn=PEER_HEADS * PEER_QDIM,
                             out_dtype=BF16, emit_xn=True, name="peer_q_proj")
        n1, e1, r2, e2 = _peer_route(qp, peer_sub_keys[l].astype(BF16))
        peer = _peer_mix(xn, peer_u[l].astype(BF16), peer_v[l].T.astype(BF16), n1, e1, r2, e2)
        last = l == depth - 1
        h = _add_norm(h, peer, norm_final if last else jnp.ones((D,), F32), norm=last)
    return h.reshape(B, S, D)
```

```python
import functools

import numpy as np
import jax
import jax.numpy as jnp
from jax import lax
from jax.experimental import pallas as pl
from jax.experimental.pallas import tpu as pltpu

F32 = jnp.float32
BF16 = jnp.bfloat16

D_MODEL = 4096
HEAD_DIM = 128
SWA_Q_HEADS = 16
SWA_KV_HEADS = 4
SWA_GROUP = SWA_Q_HEADS // SWA_KV_HEADS
SWA_WINDOW = 128
SWA_BLOCK = 128
ROPE_THETA = 10000.0
NAT_HEADS = 16
NAT_KR_MAX = 8
NAT_KC = 16
GRID_W = 64
NAT_QROWS = 2
SWA_Q_W = SWA_Q_HEADS * HEAD_DIM
SWA_KV_W = SWA_KV_HEADS * HEAD_DIM
NAT_W = NAT_HEADS * HEAD_DIM
MEM_HEADS = 4
MEM_HEAD_DIM = 256
MEM_W = MEM_HEADS * MEM_HEAD_DIM
PEER_HEADS = 8
PEER_NKEYS = 128
PEER_QDIM = 256
PEER_TOPK = 16
RMS_EPS = 1e-6
NEG_INF = -1e30

LANES = 128
VMEM_LIMIT_BYTES = 56 * 1024 * 1024

_CELLS = [(a, b) for a in range(PEER_TOPK) for b in range(PEER_TOPK) if (a + 1) * (b + 1) <= PEER_TOPK]
_N_CELLS = len(_CELLS)
_CELL_ROWS = -(-_N_CELLS // 8) * 8


def _params(sem):
    return pltpu.CompilerParams(dimension_semantics=sem, vmem_limit_bytes=VMEM_LIMIT_BYTES)


def _rms_matmul_kernel(*refs, seg_widths, norm, has_res, rope_cols, emit_xn):
    n_seg = len(seg_widths)
    pos = 0
    x_refs = refs[pos:pos + n_seg]; pos += n_seg
    g_refs = ()
    if norm:
        g_refs = refs[pos:pos + n_seg]; pos += n_seg
    w_ref = refs[pos]; pos += 1
    res_ref = None
    if has_res:
        res_ref = refs[pos]; pos += 1
    cos_ref = sin_ref = None
    if rope_cols:
        cos_ref, sin_ref = refs[pos], refs[pos + 1]; pos += 2
    o_ref = refs[pos]; pos += 1
    xn_out_ref = None
    if emit_xn:
        xn_out_ref = refs[pos]; pos += 1
    xn_ref = refs[pos]

    j = pl.program_id(1)

    @pl.when(j == 0)
    def _():
        off = 0
        for s in range(n_seg):
            x = x_refs[s][...].astype(F32)
            if norm:
                ms = jnp.mean(x * x, axis=-1, keepdims=True)
                x = (x * lax.rsqrt(ms + RMS_EPS)) * g_refs[s][...]
            xn_ref[:, off:off + seg_widths[s]] = x.astype(BF16)
            off += seg_widths[s]
        if emit_xn:
            xn_out_ref[...] = xn_ref[...]

    acc = jnp.dot(xn_ref[...], w_ref[...], preferred_element_type=F32)
    if has_res:
        acc = acc + res_ref[...]

    bn = acc.shape[1]
    if rope_cols:
        def store(n_roped_heads):
            for g in range(n_roped_heads):
                sl = slice(g * HEAD_DIM, (g + 1) * HEAD_DIM)
                rot = pltpu.roll(acc[:, sl], HEAD_DIM // 2, 1)
                o_ref[:, sl] = (acc[:, sl] * cos_ref[...] + rot * sin_ref[...]).astype(o_ref.dtype)
            if n_roped_heads * HEAD_DIM < bn:
                o_ref[:, n_roped_heads * HEAD_DIM:] = acc[:, n_roped_heads * HEAD_DIM:].astype(o_ref.dtype)

        full_blocks, rem = divmod(rope_cols, bn)
        pl.when(j < full_blocks)(lambda: store(bn // HEAD_DIM))
        if rem:
            pl.when(j == full_blocks)(lambda: store(rem // HEAD_DIM))
        pl.when(j >= full_blocks + (1 if rem else 0))(lambda: store(0))
    else:
        o_ref[...] = acc.astype(o_ref.dtype)


def _rms_matmul(xs, gains, w, *, bm, bn, out_dtype, res=None, rope=None, rope_cols=0, emit_xn=False, name):
    T = xs[0].shape[0]
    seg_widths = tuple(int(x.shape[1]) for x in xs)
    K = sum(seg_widths)
    N = w.shape[1]
    norm = gains is not None
    assert w.shape[0] == K and T % bm == 0 and N % bn == 0 and rope_cols % HEAD_DIM == 0
    in_specs = [pl.BlockSpec((bm, kw), lambda i, j: (i, 0)) for kw in seg_widths]
    args = list(xs)
    if norm:
        in_specs += [pl.BlockSpec((1, kw), lambda i, j: (0, 0)) for kw in seg_widths]
        args += [g.reshape(1, -1).astype(F32) for g in gains]
    in_specs.append(pl.BlockSpec((K, bn), lambda i, j: (0, j), pipeline_mode=pl.Buffered(1) if bn == N else None))
    args.append(w)
    if res is not None:
        in_specs.append(pl.BlockSpec((bm, bn), lambda i, j: (i, j)))
        args.append(res)
    if rope_cols:
        cos, sin = rope
        pos_blocks = cos.shape[0] // bm
        in_specs += [pl.BlockSpec((bm, HEAD_DIM), lambda i, j: (i % pos_blocks, 0))] * 2
        args += [cos, sin]
    out_shape = [jax.ShapeDtypeStruct((T, N), out_dtype)]
    out_specs = [pl.BlockSpec((bm, bn), lambda i, j: (i, j))]
    if emit_xn:
        out_shape.append(jax.ShapeDtypeStruct((T, K), BF16))
        out_specs.append(pl.BlockSpec((bm, K), lambda i, j: (i, 0)))
    kern = functools.partial(_rms_matmul_kernel, seg_widths=seg_widths, norm=norm, has_res=res is not None,
                             rope_cols=rope_cols, emit_xn=emit_xn)
    outs = pl.pallas_call(
        kern,
        grid=(T // bm, N // bn),
        in_specs=in_specs,
        out_specs=out_specs,
        out_shape=out_shape,
        scratch_shapes=[pltpu.VMEM((bm, K), BF16)],
        compiler_params=_params(("arbitrary", "arbitrary")),
        name=name,
    )(*args)
    return outs if emit_xn else outs[0]


def _swa_kernel(q_ref, k_ref, v_ref, sink_ref, o_ref, *, seq, qblocks):
    band = 3 * SWA_BLOCK
    scale = HEAD_DIM ** -0.5
    sink = jnp.concatenate([jnp.broadcast_to(sink_ref[0, g:g + 1, 0:1], (SWA_BLOCK, 1))
                            for g in range(SWA_GROUP)], axis=0)
    for t in range(qblocks):
        n = pl.program_id(2) * qblocks + t
        rows = slice(t * SWA_BLOCK, (t + 1) * SWA_BLOCK)
        start = pl.multiple_of(jnp.clip((n - 1) * SWA_BLOCK, 0, seq - band), SWA_BLOCK)
        k = k_ref[pl.ds(start, band), :]
        v = v_ref[pl.ds(start, band), :]
        qpos = n * SWA_BLOCK + lax.broadcasted_iota(jnp.int32, (SWA_BLOCK, band), 0)
        kpos = start + lax.broadcasted_iota(jnp.int32, (SWA_BLOCK, band), 1)
        valid = jnp.abs(qpos - kpos) <= SWA_WINDOW
        q = jnp.concatenate([q_ref[rows, g * HEAD_DIM:(g + 1) * HEAD_DIM] for g in range(SWA_GROUP)], axis=0)
        s = lax.dot_general(q, k, (((1,), (1,)), ((), ())), preferred_element_type=F32) * scale
        s = jnp.where(jnp.concatenate([valid] * SWA_GROUP, axis=0), s, NEG_INF)
        m = jnp.maximum(jnp.max(s, axis=-1, keepdims=True), sink)
        p = jnp.exp(s - m)
        denom = jnp.sum(p, axis=-1, keepdims=True) + jnp.exp(sink - m)
        o = jnp.dot((p / denom).astype(BF16), v, preferred_element_type=F32)
        for g in range(SWA_GROUP):
            o_ref[rows, g * HEAD_DIM:(g + 1) * HEAD_DIM] = o[g * SWA_BLOCK:(g + 1) * SWA_BLOCK, :]


def _swa_attention(qkv, sink, batch, seq, *, qblocks=2):
    T = batch * seq
    nb = seq // (SWA_BLOCK * qblocks)
    gw = SWA_GROUP * HEAD_DIM
    k_col0 = SWA_Q_W // HEAD_DIM
    v_col0 = (SWA_Q_W + SWA_KV_W) // HEAD_DIM
    sink_b = jnp.broadcast_to(sink.astype(F32).reshape(SWA_KV_HEADS, SWA_GROUP, 1),
                              (SWA_KV_HEADS, SWA_GROUP, LANES))
    return pl.pallas_call(
        functools.partial(_swa_kernel, seq=seq, qblocks=qblocks),
        grid=(batch, SWA_KV_HEADS, nb),
        in_specs=[
            pl.BlockSpec((SWA_BLOCK * qblocks, gw), lambda b, h, n: (b * nb + n, h)),
            pl.BlockSpec((seq, HEAD_DIM), lambda b, h, n: (b, k_col0 + h)),
            pl.BlockSpec((seq, HEAD_DIM), lambda b, h, n: (b, v_col0 + h)),
            pl.BlockSpec((1, SWA_GROUP, LANES), lambda b, h, n: (h, 0, 0)),
        ],
        out_specs=pl.BlockSpec((SWA_BLOCK * qblocks, gw), lambda b, h, n: (b * nb + n, h)),
        out_shape=jax.ShapeDtypeStruct((T, SWA_Q_W), F32),
        compiler_params=_params(("arbitrary", "arbitrary", "arbitrary")),
        name="swa_attention",
    )(qkv, qkv, qkv, sink_b)


def _nat_tables(seq):
    rows = seq // GRID_W
    kr = min(NAT_KR_MAX, rows)
    span = min(rows, NAT_QROWS + kr - 1)
    nb = rows // NAT_QROWS
    qbl = NAT_QROWS * GRID_W
    r0 = np.arange(nb) * NAT_QROWS
    kstart = np.clip(r0 - kr // 2, 0, rows - span)
    key_rows = kstart[:, None] + np.arange(span)[None, :]
    q_row = r0[:, None] + (np.arange(qbl) // GRID_W)[None, :]
    q_col = np.broadcast_to((np.arange(qbl) % GRID_W)[None, :], q_row.shape)
    k_row = np.repeat(key_rows, GRID_W, axis=1)
    k_col = np.broadcast_to(np.tile(np.arange(GRID_W), span)[None, :], k_row.shape)
    rs = np.clip(q_row - kr // 2, 0, rows - kr)[:, :, None]
    cs = np.clip(q_col - NAT_KC // 2, 0, GRID_W - NAT_KC)[:, :, None]
    kr3, kc3 = k_row[:, None, :], k_col[:, None, :]
    valid = (kr3 >= rs) & (kr3 < rs + kr) & (kc3 >= cs) & (kc3 < cs + NAT_KC)
    dr = np.clip(kr3 - q_row[:, :, None] + NAT_KR_MAX - 1, 0, 2 * NAT_KR_MAX - 2)
    dc = np.clip(kc3 - q_col[:, :, None] + NAT_KC - 1, 0, 2 * NAT_KC - 2)
    flat = np.where(valid, dr * (2 * NAT_KC - 1) + dc, -1).reshape(nb, -1)
    _, first, inverse = np.unique(flat, axis=0, return_index=True, return_inverse=True)
    valid, dr, dc = valid[first], dr[first], dc[first]
    dr_tab = dr[:, ::GRID_W, ::GRID_W]
    assert (dr == np.repeat(np.repeat(dr_tab, GRID_W, axis=1), GRID_W, axis=2)).all()
    col = np.arange(GRID_W)
    toeplitz = np.clip(col[None, :] - col[:, None] + NAT_KC - 1, 0, 2 * NAT_KC - 2)
    assert (dc == np.tile(toeplitz, (NAT_QROWS, span))[None]).all()
    far = np.tile(np.abs(col[None, :] - col[:, None]) >= NAT_KC, (NAT_QROWS, span))
    assert not (valid & far[None]).any()
    return dict(nb=nb, qbl=qbl, span=span, nkeys=span * GRID_W, start=(kstart * GRID_W).astype(np.int32),
                pid=np.asarray(inverse).reshape(-1).astype(np.int32), valid=valid, dr_tab=dr_tab)


def _nat_bias_kernel(rows_ref, valid_ref, o_ref, *, span):
    n_pairs = -(-span // 2)
    for qr in range(NAT_QROWS):
        rs = slice(qr * GRID_W, (qr + 1) * GRID_W)
        for pair in range(n_pairs):
            row = rows_ref[0, 0, qr * n_pairs + pair:qr * n_pairs + pair + 1, :]
            blk = pltpu.roll(jnp.broadcast_to(row, (GRID_W, LANES)), LANES - (NAT_KC - 1), 1,
                             stride=1, stride_axis=0)
            width = min(LANES, span * GRID_W - pair * LANES)
            cs = slice(pair * LANES, pair * LANES + width)
            o_ref[0, 0, rs, cs] = jnp.where(valid_ref[0, rs, cs] > 0, blk[:, :width], NEG_INF)


def _nat_bias_table(rel_bias, tab):
    span, qbl, nkeys = tab["span"], tab["qbl"], tab["nkeys"]
    n_pat = tab["valid"].shape[0]
    n_pairs = -(-span // 2)
    nrel = 2 * NAT_KC - 1
    assert 2 * GRID_W == LANES and nrel <= GRID_W
    rows = rel_bias.astype(F32)[:, tab["dr_tab"]]
    rows = jnp.pad(rows, ((0, 0), (0, 0), (0, 0), (0, 2 * n_pairs - span), (0, GRID_W - nrel)))
    rows = jnp.transpose(rows.reshape(NAT_HEADS, n_pat, NAT_QROWS * n_pairs, LANES), (1, 0, 2, 3))
    valid = jnp.asarray(tab["valid"].astype(np.float32))
    return pl.pallas_call(
        functools.partial(_nat_bias_kernel, span=span),
        grid=(n_pat, NAT_HEADS),
        in_specs=[
            pl.BlockSpec((1, 1, NAT_QROWS * n_pairs, LANES), lambda p, h: (p, h, 0, 0)),
            pl.BlockSpec((1, qbl, nkeys), lambda p, h: (p, 0, 0)),
        ],
        out_specs=pl.BlockSpec((1, 1, qbl, nkeys), lambda p, h: (p, h, 0, 0)),
        out_shape=jax.ShapeDtypeStruct((n_pat, NAT_HEADS, qbl, nkeys), F32),
        compiler_params=_params(("arbitrary", "arbitrary")),
        name="nat_bias_table",
    )(rows, valid)


def _nat_kernel(pid_ref, start_ref, q_ref, k_ref, v_ref, bias_ref, o_ref, *, nkeys, heads):
    del pid_ref
    n = pl.program_id(2)
    start = pl.multiple_of(start_ref[n], GRID_W)
    scale = HEAD_DIM ** -0.5
    sls = [slice(h * HEAD_DIM, (h + 1) * HEAD_DIM) for h in range(heads)]
    q = jnp.stack([q_ref[:, sl] for sl in sls], axis=0)
    k = jnp.stack([k_ref[pl.ds(start, nkeys), sl] for sl in sls], axis=0)
    v = jnp.stack([v_ref[pl.ds(start, nkeys), sl] for sl in sls], axis=0)
    s = lax.dot_general(q, k, (((2,), (2,)), ((0,), (0,))), preferred_element_type=F32) * scale
    s = s + bias_ref[0]
    m = jnp.max(s, axis=-1, keepdims=True)
    p = jnp.exp(s - m)
    p = (p / jnp.sum(p, axis=-1, keepdims=True)).astype(BF16)
    o = lax.dot_general(p, v, (((2,), (1,)), ((0,), (0,))), preferred_element_type=F32)
    for h, sl in enumerate(sls):
        o_ref[:, sl] = o[h]


def _nat_attention(qkv, rel_bias, batch, seq):
    T = batch * seq
    tab = _nat_tables(seq)
    nb, qbl, nkeys = tab["nb"], tab["qbl"], tab["nkeys"]
    bias = _nat_bias_table(rel_bias, tab)
    hpg = 4
    gw = hpg * HEAD_DIM
    n_hg = NAT_HEADS // hpg
    q0 = (SWA_Q_W + 2 * SWA_KV_W) // gw
    k0 = q0 + n_hg
    v0 = k0 + n_hg
    grid_spec = pltpu.PrefetchScalarGridSpec(
        num_scalar_prefetch=2,
        grid=(batch, n_hg, nb),
        in_specs=[
            pl.BlockSpec((qbl, gw), lambda b, g, n, pid, st: (b * nb + n, q0 + g)),
            pl.BlockSpec((seq, gw), lambda b, g, n, pid, st: (b, k0 + g)),
            pl.BlockSpec((seq, gw), lambda b, g, n, pid, st: (b, v0 + g)),
            pl.BlockSpec((1, hpg, qbl, nkeys), lambda b, g, n, pid, st: (pid[n], g, 0, 0)),
        ],
        out_specs=pl.BlockSpec((qbl, gw), lambda b, g, n, pid, st: (b * nb + n, g)),
    )
    return pl.pallas_call(
        functools.partial(_nat_kernel, nkeys=nkeys, heads=hpg),
        grid_spec=grid_spec,
        out_shape=jax.ShapeDtypeStruct((T, NAT_W), F32),
        compiler_params=_params(("arbitrary", "arbitrary", "arbitrary")),
        name="nat_attention",
    )(jnp.asarray(tab["pid"]), jnp.asarray(tab["start"]), qkv, qkv, qkv, bias)


def _mem_attn_kernel(q_ref, kv_ref, o_ref):
    scale = MEM_HEAD_DIM ** -0.5
    for h in range(MEM_HEADS):
        sl = slice(h * MEM_HEAD_DIM, (h + 1) * MEM_HEAD_DIM)
        q = q_ref[:, sl]
        k = kv_ref[:, sl]
        v = kv_ref[:, MEM_W + h * MEM_HEAD_DIM:MEM_W + (h + 1) * MEM_HEAD_DIM]
        s = lax.dot_general(q, k, (((1,), (1,)), ((), ())), preferred_element_type=F32) * scale
        m = jnp.max(s, axis=-1, keepdims=True)
        p = jnp.exp(s - m)
        p = (p / jnp.sum(p, axis=-1, keepdims=True)).astype(BF16)
        o_ref[:, sl] = jnp.dot(p, v, preferred_element_type=F32).astype(o_ref.dtype)


def _mem_attention(q, kv, batch, seq, mem_len, *, bq=512):
    T = batch * seq
    nq = seq // bq
    return pl.pallas_call(
        _mem_attn_kernel,
        grid=(batch, nq),
        in_specs=[
            pl.BlockSpec((bq, MEM_W), lambda b, i: (b * nq + i, 0)),
            pl.BlockSpec((mem_len, 2 * MEM_W), lambda b, i: (b, 0)),
        ],
        out_specs=pl.BlockSpec((bq, MEM_W), lambda b, i: (b * nq + i, 0)),
        out_shape=jax.ShapeDtypeStruct((T, MEM_W), BF16),
        compiler_params=_params(("arbitrary", "arbitrary")),
        name="mem_attention",
    )(q, kv)


def _extract_max(work, rows):
    m = jnp.max(work, axis=0, keepdims=True)
    first = jnp.min(jnp.where(work == m, rows, float(work.shape[0])), axis=0, keepdims=True)
    return m, rows == first


def _top16_rows(scores, dests, work_scr, rank_scr, base):
    shape = scores[0].shape
    rows = lax.broadcasted_iota(jnp.int32, shape, 0).astype(F32)
    chains = range(len(scores))

    def run(first_only):
        for n, s in enumerate(scores):
            work_scr[base + n] = s
            rank_scr[base + n] = jnp.full(shape, float(PEER_TOPK), F32)

        def body(a, carry):
            for n in chains:
                work = work_scr[base + n]
                if first_only:
                    m, hit = _extract_max(work, rows)
                else:
                    m = jnp.max(work, axis=0, keepdims=True)
                    hit = work == m
                ref, idx = dests[n]
                ref[idx, pl.ds(a, 1), :] = m
                work_scr[base + n] = jnp.where(hit, -jnp.inf, work)
                rank_scr[base + n] = jnp.where(hit, jnp.asarray(a, F32), rank_scr[base + n])
            return carry

        lax.fori_loop(0, PEER_TOPK, body, 0)

    run(first_only=False)
    excess = jnp.zeros((1, shape[1]), F32)
    for n in chains:
        taken = jnp.sum(jnp.where(rank_scr[base + n] < float(PEER_TOPK), 1.0, 0.0), axis=0, keepdims=True)
        excess = jnp.maximum(excess, jnp.abs(taken - float(PEER_TOPK)))
    pl.when(jnp.max(excess) > 0.0)(lambda: run(first_only=True))
    return [rank_scr[base + n] for n in chains]


def _route_kernel(q_ref, keys_ref, n1_ref, e1_ref, r2_ref, e2_ref, v1_scr, v2_scr, c_scr, sel_scr, work_scr, rank_scr,
                  *, tb, group):
    half = PEER_QDIM // 2
    dn = (((1,), (1,)), ((), ()))
    s1_all = lax.dot_general(keys_ref[0, 0], q_ref[:, :half], dn, preferred_element_type=F32)
    s2_all = lax.dot_general(keys_ref[0, 1], q_ref[:, half:], dn, preferred_element_type=F32)
    n_c = tb // LANES
    cols = [slice(c * LANES, (c + 1) * LANES) for c in range(n_c)]
    s1 = [s1_all[:, cl] for cl in cols]
    s2 = [s2_all[:, cl] for cl in cols]
    rank1, rank2 = [], []
    for c0 in range(0, n_c, group):
        cs = list(range(c0, c0 + group))
        ranks = _top16_rows([s1[c] for c in cs] + [s2[c] for c in cs],
                            [(v1_scr, c) for c in cs] + [(v2_scr, c) for c in cs], work_scr, rank_scr, 2 * c0)
        rank1 += ranks[:group]
        rank2 += ranks[group:]
    for c in range(n_c):
        for p, (a, b) in enumerate(_CELLS):
            c_scr[c, p:p + 1, :] = v1_scr[c, a:a + 1, :] + v2_scr[c, b:b + 1, :]
        if _CELL_ROWS > _N_CELLS:
            c_scr[c, _N_CELLS:, :] = jnp.full((_CELL_ROWS - _N_CELLS, LANES), -jnp.inf, F32)
    cand = [c_scr[c] for c in range(n_c)]
    prow = lax.broadcasted_iota(jnp.int32, cand[0].shape, 0).astype(F32)

    def body(t, carry):
        out = []
        for work, picked in carry:
            _, hit = _extract_max(work, prow)
            out.append((jnp.where(hit, -jnp.inf, work), jnp.where(hit, 1.0, picked)))
        return tuple(out)

    picked = [pk for _, pk in lax.fori_loop(0, PEER_TOPK, body,
                                            tuple((cd, jnp.zeros(cd.shape, F32)) for cd in cand))]
    for c in range(n_c):
        top1 = v1_scr[c, 0:1, :]
        top2 = v2_scr[c, 0:1, :]
        z = jnp.sum(jnp.where(picked[c] > 0, jnp.exp(cand[c] - (top1 + top2)), 0.0), axis=0, keepdims=True)
        sel_scr[c] = picked[c]
        n1 = jnp.zeros(s1[c].shape, F32)
        p0 = 0
        for a in range(PEER_TOPK):
            width = sum(1 for (aa, _) in _CELLS if aa == a)
            n_a = jnp.sum(sel_scr[c, p0:p0 + width, :], axis=0, keepdims=True)
            n1 = n1 + jnp.where(rank1[c] == float(a), n_a, 0.0)
            p0 += width
        n1_ref[0, :, cols[c]] = n1
        e1_ref[0, :, cols[c]] = jnp.exp(s1[c] - top1) / z
        r2_ref[0, :, cols[c]] = rank2[c]
        e2_ref[0, :, cols[c]] = jnp.exp(s2[c] - top2)


def _peer_route(q, sub_keys, *, tb=512, group=2):
    T = q.shape[0]
    shape = jax.ShapeDtypeStruct((PEER_HEADS, PEER_NKEYS, T), F32)
    out_spec = pl.BlockSpec((1, PEER_NKEYS, tb), lambda i, h: (h, 0, i))
    return pl.pallas_call(
        functools.partial(_route_kernel, tb=tb, group=group),
        grid=(T // tb, PEER_HEADS),
        in_specs=[
            pl.BlockSpec((tb, PEER_QDIM), lambda i, h: (i, h)),
            pl.BlockSpec((1, 2, PEER_NKEYS, PEER_QDIM // 2), lambda i, h: (h, 0, 0, 0)),
        ],
        out_specs=[out_spec] * 4,
        out_shape=[shape] * 4,
        scratch_shapes=[pltpu.VMEM((tb // LANES, PEER_TOPK, LANES), F32),
                        pltpu.VMEM((tb // LANES, PEER_TOPK, LANES), F32),
                        pltpu.VMEM((tb // LANES, _CELL_ROWS, LANES), F32),
                        pltpu.VMEM((tb // LANES, _CELL_ROWS, LANES), F32),
                        pltpu.VMEM((2 * (tb // LANES), PEER_NKEYS, LANES), F32),
                        pltpu.VMEM((2 * (tb // LANES), PEER_NKEYS, LANES), F32)],
        compiler_params=_params(("arbitrary", "arbitrary")),
        name="peer_route",
    )(q, sub_keys)


def _peer_kernel(xn_ref, u_ref, vt_ref, n1_ref, e1_ref, r2_ref, e2_ref, o_ref, a_scr, w_scr, *, eb, tc):
    k = pl.program_id(1)
    nt = (((1,), (1,)), ((), ()))

    @pl.when(k == 0)
    def _():
        o_ref[...] = jnp.zeros(o_ref.shape, F32)
        a_scr[...] = lax.dot_general(u_ref[...], xn_ref[...], nt, preferred_element_type=F32)

    @pl.when(k > 0)
    def _():
        n_i = eb // PEER_NKEYS
        for c in range(a_scr.shape[1] // tc):
            cols = slice(c * tc, (c + 1) * tc)
            for a in range(n_i):
                rows = slice(a * PEER_NKEYS, (a + 1) * PEER_NKEYS)
                i_row = k * n_i + (a - n_i)
                a_t = a_scr[rows, cols]
                act = 0.5 * a_t * (1.0 + lax.erf(a_t * np.float32(np.sqrt(0.5))))
                gate = jnp.zeros((PEER_NKEYS, tc), F32)
                for h in range(PEER_HEADS):
                    cnt = n1_ref[h, pl.ds(i_row, 1), :][:, cols]
                    g1 = e1_ref[h, pl.ds(i_row, 1), :][:, cols]
                    gate = gate + jnp.where(r2_ref[h, :, cols] < cnt, e2_ref[h, :, cols] * g1, 0.0)
                w_scr[rows, cols] = (act * gate).astype(BF16)
        a_scr[...] = lax.dot_general(u_ref[...], xn_ref[...], nt, preferred_element_type=F32)
        o_ref[...] += jnp.dot(vt_ref[0], w_scr[...], preferred_element_type=F32)


def _peer_mix(xn, u, vt, n1, e1, r2, e2, *, tb=512, tc=128):
    T, D = xn.shape
    E = u.shape[0]
    n_e, _, eb = vt.shape
    once = pl.Buffered(1)
    route_spec = pl.BlockSpec((PEER_HEADS, PEER_NKEYS, tb), lambda i, k: (0, 0, i), pipeline_mode=once)
    return pl.pallas_call(
        functools.partial(_peer_kernel, eb=eb, tc=tc),
        grid=(T // tb, n_e + 1),
        in_specs=[
            pl.BlockSpec((tb, D), lambda i, k: (i, 0), pipeline_mode=once),
            pl.BlockSpec((eb, D), lambda i, k: (jnp.minimum(k, n_e - 1), 0)),
            pl.BlockSpec((1, D, eb), lambda i, k: (jnp.maximum(k - 1, 0), 0, 0)),
            route_spec, route_spec, route_spec, route_spec,
        ],
        out_specs=pl.BlockSpec((D, tb), lambda i, k: (0, i)),
        out_shape=jax.ShapeDtypeStruct((D, T), F32),
        scratch_shapes=[pltpu.VMEM((eb, tb), F32), pltpu.VMEM((eb, tb), BF16)],
        compiler_params=_params(("arbitrary", "arbitrary")),
        name="peer_mix",
    )(xn, u, vt, n1, e1, r2, e2)


def _add_norm_kernel(a_ref, bt_ref, g_ref, o_ref, *, norm):
    x = a_ref[...] + bt_ref[...].T
    if norm:
        ms = jnp.mean(x * x, axis=-1, keepdims=True)
        x = (x * lax.rsqrt(ms + RMS_EPS)) * g_ref[...]
    o_ref[...] = x


def _add_norm(a, bt, gain, *, norm, bm=256):
    T, D = a.shape
    spec = pl.BlockSpec((bm, D), lambda i: (i, 0))
    return pl.pallas_call(
        functools.partial(_add_norm_kernel, norm=norm),
        grid=(T // bm,),
        in_specs=[spec, pl.BlockSpec((D, bm), lambda i: (0, i)), pl.BlockSpec((1, D), lambda i: (0, 0))],
        out_specs=spec,
        out_shape=jax.ShapeDtypeStruct((T, D), F32),
        compiler_params=_params(("arbitrary",)),
        name="add_norm",
    )(a, bt, gain.reshape(1, D).astype(F32))


def _rope_tables(seq):
    half = HEAD_DIM // 2
    inv = ROPE_THETA ** (-jnp.arange(half, dtype=F32) / half)
    ang = jnp.arange(seq, dtype=jnp.int32).astype(F32)[:, None] * inv[None, :]
    cos, sin = jnp.cos(ang), jnp.sin(ang)
    return jnp.concatenate([cos, cos], axis=-1), jnp.concatenate([-sin, sin], axis=-1)


def kernel(x, mem, norm_mix, w_in, swa_sink, nat_rel_bias, out_norm_swa, out_norm_nat, w_out, norm_mem_q,
           norm_mem_kv, w_mem_q, w_mem_k, w_mem_v, w_mem_o, norm_ffn, w_peer_q, peer_sub_keys, peer_u, peer_v,
           norm_final):
    B, S, D = x.shape
    M = mem.shape[1]
    T = B * S
    depth = w_in.shape[0]
    assert depth >= 1
    rope = _rope_tables(S)
    h = x.reshape(T, D)
    mem2 = mem.reshape(B * M, D)
    rope_cols = SWA_Q_W + SWA_KV_W
    for l in range(depth):
        qkv = _rms_matmul([h], [norm_mix[l]], w_in[l].astype(BF16), bm=512, bn=1024, out_dtype=BF16,
                          rope=rope, rope_cols=rope_cols, name="in_proj")
        oa = _swa_attention(qkv, swa_sink[l], B, S)
        ob = _nat_attention(qkv, nat_rel_bias[l], B, S)
        h = _rms_matmul([oa, ob], [out_norm_swa[l], out_norm_nat[l]], w_out[l].astype(BF16), bm=512, bn=1024,
                        out_dtype=F32, res=h, name="out_proj")
        w_kv = jnp.concatenate([w_mem_k[l], w_mem_v[l]], axis=1).astype(BF16)
        kv = _rms_matmul([mem2], [norm_mem_kv[l]], w_kv, bm=512, bn=512, out_dtype=BF16, name="mem_kv_proj")
        qm = _rms_matmul([h], [norm_mem_q[l]], w_mem_q[l].astype(BF16), bm=512, bn=MEM_W, out_dtype=BF16,
                         name="mem_q_proj")
        om = _mem_attention(qm, kv, B, S, M)
        h = _rms_matmul([om], None, w_mem_o[l].astype(BF16), bm=512, bn=D, out_dtype=F32, res=h,
                        name="mem_o_proj")
        qp, xn = _rms_matmul([h], [norm_ffn[l]], w_peer_q[l].astype(BF16), bm=512, bn=PEER_HEADS * PEER_QDIM,
                             out_dtype=BF16, emit_xn=True, name="peer_q_proj")
        n1, e1, r2, e2 = _peer_route(qp, peer_sub_keys[l].astype(BF16))
        eb = 512
        vt = jnp.transpose(peer_v[l].reshape(-1, eb, D), (0, 2, 1)).astype(BF16)
        peer = _peer_mix(xn, peer_u[l].astype(BF16), vt, n1, e1, r2, e2)
        last = l == depth - 1
        h = _add_norm(h, peer, norm_final if last else jnp.ones((D,), F32), norm=last)
    return h.reshape(B, S, D)
```

```python
import functools

import numpy as np
import jax
import jax.numpy as jnp
from jax import lax
from jax.experimental import pallas as pl
from jax.experimental.pallas import tpu as pltpu

F32 = jnp.float32
BF16 = jnp.bfloat16

D_MODEL = 4096
HEAD_DIM = 128
SWA_Q_HEADS = 16
SWA_KV_HEADS = 4
SWA_GROUP = SWA_Q_HEADS // SWA_KV_HEADS
SWA_WINDOW = 128
SWA_BLOCK = 128
ROPE_THETA = 10000.0
NAT_HEADS = 16
NAT_KR_MAX = 8
NAT_KC = 16
GRID_W = 64
NAT_QROWS = 2
SWA_Q_W = SWA_Q_HEADS * HEAD_DIM
SWA_KV_W = SWA_KV_HEADS * HEAD_DIM
NAT_W = NAT_HEADS * HEAD_DIM
MEM_HEADS = 4
MEM_HEAD_DIM = 256
MEM_W = MEM_HEADS * MEM_HEAD_DIM
PEER_HEADS = 8
PEER_NKEYS = 128
PEER_QDIM = 256
PEER_TOPK = 16
RMS_EPS = 1e-6
NEG_INF = -1e30

LANES = 128
VMEM_LIMIT_BYTES = 56 * 1024 * 1024

_CELLS = [(a, b) for a in range(PEER_TOPK) for b in range(PEER_TOPK) if (a + 1) * (b + 1) <= PEER_TOPK]
_N_CELLS = len(_CELLS)
_CELL_ROWS = -(-_N_CELLS // 8) * 8


def _params(sem):
    return pltpu.CompilerParams(dimension_semantics=sem, vmem_limit_bytes=VMEM_LIMIT_BYTES)


def _rms_matmul_kernel(*refs, seg_widths, norm, has_res, rope_cols, emit_xn):
    n_seg = len(seg_widths)
    pos = 0
    x_refs = refs[pos:pos + n_seg]; pos += n_seg
    g_refs = ()
    if norm:
        g_refs = refs[pos:pos + n_seg]; pos += n_seg
    w_ref = refs[pos]; pos += 1
    res_ref = None
    if has_res:
        res_ref = refs[pos]; pos += 1
    cos_ref = sin_ref = None
    if rope_cols:
        cos_ref, sin_ref = refs[pos], refs[pos + 1]; pos += 2
    o_ref = refs[pos]; pos += 1
    xn_out_ref = None
    if emit_xn:
        xn_out_ref = refs[pos]; pos += 1
    xn_ref = refs[pos]

    j = pl.program_id(1)

    @pl.when(j == 0)
    def _():
        off = 0
        for s in range(n_seg):
            x = x_refs[s][...].astype(F32)
            if norm:
                ms = jnp.mean(x * x, axis=-1, keepdims=True)
                x = (x * lax.rsqrt(ms + RMS_EPS)) * g_refs[s][...]
            xn_ref[:, off:off + seg_widths[s]] = x.astype(BF16)
            off += seg_widths[s]
        if emit_xn:
            xn_out_ref[...] = xn_ref[...]

    acc = jnp.dot(xn_ref[...], w_ref[...], preferred_element_type=F32)
    if has_res:
        acc = acc + res_ref[...]

    bn = acc.shape[1]
    if rope_cols:
        def store(n_roped_heads):
            for g in range(n_roped_heads):
                sl = slice(g * HEAD_DIM, (g + 1) * HEAD_DIM)
                rot = pltpu.roll(acc[:, sl], HEAD_DIM // 2, 1)
                o_ref[:, sl] = (acc[:, sl] * cos_ref[...] + rot * sin_ref[...]).astype(o_ref.dtype)
            if n_roped_heads * HEAD_DIM < bn:
                o_ref[:, n_roped_heads * HEAD_DIM:] = acc[:, n_roped_heads * HEAD_DIM:].astype(o_ref.dtype)

        full_blocks, rem = divmod(rope_cols, bn)
        pl.when(j < full_blocks)(lambda: store(bn // HEAD_DIM))
        if rem:
            pl.when(j == full_blocks)(lambda: store(rem // HEAD_DIM))
        pl.when(j >= full_blocks + (1 if rem else 0))(lambda: store(0))
    else:
        o_ref[...] = acc.astype(o_ref.dtype)


def _rms_matmul(xs, gains, w, *, bm, bn, out_dtype, res=None, rope=None, rope_cols=0, emit_xn=False, name):
    T = xs[0].shape[0]
    seg_widths = tuple(int(x.shape[1]) for x in xs)
    K = sum(seg_widths)
    N = w.shape[1]
    norm = gains is not None
    assert w.shape[0] == K and T % bm == 0 and N % bn == 0 and rope_cols % HEAD_DIM == 0
    in_specs = [pl.BlockSpec((bm, kw), lambda i, j: (i, 0)) for kw in seg_widths]
    args = list(xs)
    if norm:
        in_specs += [pl.BlockSpec((1, kw), lambda i, j: (0, 0)) for kw in seg_widths]
        args += [g.reshape(1, -1).astype(F32) for g in gains]
    in_specs.append(pl.BlockSpec((K, bn), lambda i, j: (0, j), pipeline_mode=pl.Buffered(1) if bn == N else None))
    args.append(w)
    if res is not None:
        in_specs.append(pl.BlockSpec((bm, bn), lambda i, j: (i, j)))
        args.append(res)
    if rope_cols:
        cos, sin = rope
        pos_blocks = cos.shape[0] // bm
        in_specs += [pl.BlockSpec((bm, HEAD_DIM), lambda i, j: (i % pos_blocks, 0))] * 2
        args += [cos, sin]
    out_shape = [jax.ShapeDtypeStruct((T, N), out_dtype)]
    out_specs = [pl.BlockSpec((bm, bn), lambda i, j: (i, j))]
    if emit_xn:
        out_shape.append(jax.ShapeDtypeStruct((T, K), BF16))
        out_specs.append(pl.BlockSpec((bm, K), lambda i, j: (i, 0)))
    kern = functools.partial(_rms_matmul_kernel, seg_widths=seg_widths, norm=norm, has_res=res is not None,
                             rope_cols=rope_cols, emit_xn=emit_xn)
    outs = pl.pallas_call(
        kern,
        grid=(T // bm, N // bn),
        in_specs=in_specs,
        out_specs=out_specs,
        out_shape=out_shape,
        scratch_shapes=[pltpu.VMEM((bm, K), BF16)],
        compiler_params=_params(("arbitrary", "arbitrary")),
        name=name,
    )(*args)
    return outs if emit_xn else outs[0]


def _swa_kernel(q_ref, k_ref, v_ref, sink_ref, o_ref, *, seq, qblocks):
    band = 3 * SWA_BLOCK
    scale = HEAD_DIM ** -0.5
    sink = jnp.concatenate([jnp.broadcast_to(sink_ref[0, g:g + 1, 0:1], (SWA_BLOCK, 1))
                            for g in range(SWA_GROUP)], axis=0)
    blocks = range(qblocks)
    rows = [slice(t * SWA_BLOCK, (t + 1) * SWA_BLOCK) for t in blocks]
    scores, values = [], []
    for t in blocks:
        n = pl.program_id(2) * qblocks + t
        start = pl.multiple_of(jnp.clip((n - 1) * SWA_BLOCK, 0, seq - band), SWA_BLOCK)
        k = k_ref[pl.ds(start, band), :]
        values.append(v_ref[pl.ds(start, band), :])
        qpos = n * SWA_BLOCK + lax.broadcasted_iota(jnp.int32, (SWA_BLOCK, band), 0)
        kpos = start + lax.broadcasted_iota(jnp.int32, (SWA_BLOCK, band), 1)
        valid = jnp.abs(qpos - kpos) <= SWA_WINDOW
        q = jnp.concatenate([q_ref[rows[t], g * HEAD_DIM:(g + 1) * HEAD_DIM] for g in range(SWA_GROUP)], axis=0)
        s = lax.dot_general(q, k, (((1,), (1,)), ((), ())), preferred_element_type=F32) * scale
        scores.append(jnp.where(jnp.concatenate([valid] * SWA_GROUP, axis=0), s, NEG_INF))
    maxes = [jnp.maximum(jnp.max(s, axis=-1, keepdims=True), sink) for s in scores]
    probs = [jnp.exp(s - m) for s, m in zip(scores, maxes)]
    denoms = [jnp.sum(p, axis=-1, keepdims=True) + jnp.exp(sink - m) for p, m in zip(probs, maxes)]
    outs = [jnp.dot((p * (1.0 / d)).astype(BF16), v, preferred_element_type=F32)
            for p, d, v in zip(probs, denoms, values)]
    for t in blocks:
        for g in range(SWA_GROUP):
            o_ref[rows[t], g * HEAD_DIM:(g + 1) * HEAD_DIM] = outs[t][g * SWA_BLOCK:(g + 1) * SWA_BLOCK, :]


def _swa_attention(qkv, sink, batch, seq, *, qblocks=4):
    T = batch * seq
    nb = seq // (SWA_BLOCK * qblocks)
    gw = SWA_GROUP * HEAD_DIM
    k_col0 = SWA_Q_W // HEAD_DIM
    v_col0 = (SWA_Q_W + SWA_KV_W) // HEAD_DIM
    sink_b = jnp.broadcast_to(sink.astype(F32).reshape(SWA_KV_HEADS, SWA_GROUP, 1),
                              (SWA_KV_HEADS, SWA_GROUP, LANES))
    return pl.pallas_call(
        functools.partial(_swa_kernel, seq=seq, qblocks=qblocks),
        grid=(batch, SWA_KV_HEADS, nb),
        in_specs=[
            pl.BlockSpec((SWA_BLOCK * qblocks, gw), lambda b, h, n: (b * nb + n, h)),
            pl.BlockSpec((seq, HEAD_DIM), lambda b, h, n: (b, k_col0 + h)),
            pl.BlockSpec((seq, HEAD_DIM), lambda b, h, n: (b, v_col0 + h)),
            pl.BlockSpec((1, SWA_GROUP, LANES), lambda b, h, n: (h, 0, 0)),
        ],
        out_specs=pl.BlockSpec((SWA_BLOCK * qblocks, gw), lambda b, h, n: (b * nb + n, h)),
        out_shape=jax.ShapeDtypeStruct((T, SWA_Q_W), F32),
        compiler_params=_params(("arbitrary", "arbitrary", "arbitrary")),
        name="swa_attention",
    )(qkv, qkv, qkv, sink_b)


def _nat_tables(seq):
    rows = seq // GRID_W
    kr = min(NAT_KR_MAX, rows)
    span = min(rows, NAT_QROWS + kr - 1)
    nb = rows // NAT_QROWS
    qbl = NAT_QROWS * GRID_W
    r0 = np.arange(nb) * NAT_QROWS
    kstart = np.clip(r0 - kr // 2, 0, rows - span)
    key_rows = kstart[:, None] + np.arange(span)[None, :]
    q_row = r0[:, None] + (np.arange(qbl) // GRID_W)[None, :]
    q_col = np.broadcast_to((np.arange(qbl) % GRID_W)[None, :], q_row.shape)
    k_row = np.repeat(key_rows, GRID_W, axis=1)
    k_col = np.broadcast_to(np.tile(np.arange(GRID_W), span)[None, :], k_row.shape)
    rs = np.clip(q_row - kr // 2, 0, rows - kr)[:, :, None]
    cs = np.clip(q_col - NAT_KC // 2, 0, GRID_W - NAT_KC)[:, :, None]
    kr3, kc3 = k_row[:, None, :], k_col[:, None, :]
    valid = (kr3 >= rs) & (kr3 < rs + kr) & (kc3 >= cs) & (kc3 < cs + NAT_KC)
    dr = np.clip(kr3 - q_row[:, :, None] + NAT_KR_MAX - 1, 0, 2 * NAT_KR_MAX - 2)
    dc = np.clip(kc3 - q_col[:, :, None] + NAT_KC - 1, 0, 2 * NAT_KC - 2)
    flat = np.where(valid, dr * (2 * NAT_KC - 1) + dc, -1).reshape(nb, -1)
    _, first, inverse = np.unique(flat, axis=0, return_index=True, return_inverse=True)
    valid, dr, dc = valid[first], dr[first], dc[first]
    dr_tab = dr[:, ::GRID_W, ::GRID_W]
    assert (dr == np.repeat(np.repeat(dr_tab, GRID_W, axis=1), GRID_W, axis=2)).all()
    col = np.arange(GRID_W)
    toeplitz = np.clip(col[None, :] - col[:, None] + NAT_KC - 1, 0, 2 * NAT_KC - 2)
    assert (dc == np.tile(toeplitz, (NAT_QROWS, span))[None]).all()
    far = np.tile(np.abs(col[None, :] - col[:, None]) >= NAT_KC, (NAT_QROWS, span))
    assert not (valid & far[None]).any()
    return dict(nb=nb, qbl=qbl, span=span, nkeys=span * GRID_W, start=(kstart * GRID_W).astype(np.int32),
                pid=np.asarray(inverse).reshape(-1).astype(np.int32), valid=valid, dr_tab=dr_tab)


def _nat_bias_kernel(rows_ref, valid_ref, o_ref, *, span):
    n_pairs = -(-span // 2)
    for qr in range(NAT_QROWS):
        rs = slice(qr * GRID_W, (qr + 1) * GRID_W)
        for pair in range(n_pairs):
            row = rows_ref[0, 0, qr * n_pairs + pair:qr * n_pairs + pair + 1, :]
            blk = pltpu.roll(jnp.broadcast_to(row, (GRID_W, LANES)), LANES - (NAT_KC - 1), 1,
                             stride=1, stride_axis=0)
            width = min(LANES, span * GRID_W - pair * LANES)
            cs = slice(pair * LANES, pair * LANES + width)
            o_ref[0, 0, rs, cs] = jnp.where(valid_ref[0, rs, cs] > 0, blk[:, :width], NEG_INF)


def _nat_bias_table(rel_bias, tab):
    span, qbl, nkeys = tab["span"], tab["qbl"], tab["nkeys"]
    n_pat = tab["valid"].shape[0]
    n_pairs = -(-span // 2)
    nrel = 2 * NAT_KC - 1
    assert 2 * GRID_W == LANES and nrel <= GRID_W
    rows = rel_bias.astype(F32)[:, tab["dr_tab"]]
    rows = jnp.pad(rows, ((0, 0), (0, 0), (0, 0), (0, 2 * n_pairs - span), (0, GRID_W - nrel)))
    rows = jnp.transpose(rows.reshape(NAT_HEADS, n_pat, NAT_QROWS * n_pairs, LANES), (1, 0, 2, 3))
    valid = jnp.asarray(tab["valid"].astype(np.float32))
    return pl.pallas_call(
        functools.partial(_nat_bias_kernel, span=span),
        grid=(n_pat, NAT_HEADS),
        in_specs=[
            pl.BlockSpec((1, 1, NAT_QROWS * n_pairs, LANES), lambda p, h: (p, h, 0, 0)),
            pl.BlockSpec((1, qbl, nkeys), lambda p, h: (p, 0, 0)),
        ],
        out_specs=pl.BlockSpec((1, 1, qbl, nkeys), lambda p, h: (p, h, 0, 0)),
        out_shape=jax.ShapeDtypeStruct((n_pat, NAT_HEADS, qbl, nkeys), F32),
        compiler_params=_params(("arbitrary", "arbitrary")),
        name="nat_bias_table",
    )(rows, valid)


def _nat_kernel(pid_ref, start_ref, q_ref, k_ref, v_ref, *rest, nkeys, heads, qbl):
    del pid_ref
    bias_refs, o_ref = rest[:-1], rest[-1]
    scale = HEAD_DIM ** -0.5
    sls = [slice(h * HEAD_DIM, (h + 1) * HEAD_DIM) for h in range(heads)]
    blocks = range(len(bias_refs))
    rows = [slice(t * qbl, (t + 1) * qbl) for t in blocks]
    scores, values = [], []
    for t in blocks:
        start = pl.multiple_of(start_ref[pl.program_id(2) * len(bias_refs) + t], GRID_W)
        q = jnp.stack([q_ref[rows[t], sl] for sl in sls], axis=0)
        k = jnp.stack([k_ref[pl.ds(start, nkeys), sl] for sl in sls], axis=0)
        values.append(jnp.stack([v_ref[pl.ds(start, nkeys), sl] for sl in sls], axis=0))
        s = lax.dot_general(q, k, (((2,), (2,)), ((0,), (0,))), preferred_element_type=F32) * scale
        scores.append(s + bias_refs[t][0])
    maxes = [jnp.max(s, axis=-1, keepdims=True) for s in scores]
    probs = [jnp.exp(s - m) for s, m in zip(scores, maxes)]
    probs = [(p * (1.0 / jnp.sum(p, axis=-1, keepdims=True))).astype(BF16) for p in probs]
    outs = [lax.dot_general(p, v, (((2,), (1,)), ((0,), (0,))), preferred_element_type=F32)
            for p, v in zip(probs, values)]
    for t in blocks:
        for h, sl in enumerate(sls):
            o_ref[rows[t], sl] = outs[t][h]


def _nat_attention(qkv, rel_bias, batch, seq, *, qblocks=4):
    T = batch * seq
    tab = _nat_tables(seq)
    qbl, nkeys = tab["qbl"], tab["nkeys"]
    nb = tab["nb"] // qblocks
    bias = _nat_bias_table(rel_bias, tab)
    hpg = 4
    gw = hpg * HEAD_DIM
    n_hg = NAT_HEADS // hpg
    q0 = (SWA_Q_W + 2 * SWA_KV_W) // gw
    k0 = q0 + n_hg
    v0 = k0 + n_hg

    def bias_spec(t):
        return pl.BlockSpec((1, hpg, qbl, nkeys), lambda b, g, n, pid, st: (pid[n * qblocks + t], g, 0, 0))

    grid_spec = pltpu.PrefetchScalarGridSpec(
        num_scalar_prefetch=2,
        grid=(batch, n_hg, nb),
        in_specs=[
            pl.BlockSpec((qbl * qblocks, gw), lambda b, g, n, pid, st: (b * nb + n, q0 + g)),
            pl.BlockSpec((seq, gw), lambda b, g, n, pid, st: (b, k0 + g)),
            pl.BlockSpec((seq, gw), lambda b, g, n, pid, st: (b, v0 + g)),
        ] + [bias_spec(t) for t in range(qblocks)],
        out_specs=pl.BlockSpec((qbl * qblocks, gw), lambda b, g, n, pid, st: (b * nb + n, g)),
    )
    return pl.pallas_call(
        functools.partial(_nat_kernel, nkeys=nkeys, heads=hpg, qbl=qbl),
        grid_spec=grid_spec,
        out_shape=jax.ShapeDtypeStruct((T, NAT_W), F32),
        compiler_params=_params(("arbitrary", "arbitrary", "arbitrary")),
        name="nat_attention",
    )(jnp.asarray(tab["pid"]), jnp.asarray(tab["start"]), qkv, qkv, qkv, *([bias] * qblocks))


def _mem_attn_kernel(q_ref, kv_ref, o_ref):
    scale = MEM_HEAD_DIM ** -0.5
    for h in range(MEM_HEADS):
        sl = slice(h * MEM_HEAD_DIM, (h + 1) * MEM_HEAD_DIM)
        q = q_ref[:, sl]
        k = kv_ref[:, sl]
        v = kv_ref[:, MEM_W + h * MEM_HEAD_DIM:MEM_W + (h + 1) * MEM_HEAD_DIM]
        s = lax.dot_general(q, k, (((1,), (1,)), ((), ())), preferred_element_type=F32) * scale
        m = jnp.max(s, axis=-1, keepdims=True)
        p = jnp.exp(s - m)
        p = (p / jnp.sum(p, axis=-1, keepdims=True)).astype(BF16)
        o_ref[:, sl] = jnp.dot(p, v, preferred_element_type=F32).astype(o_ref.dtype)


def _mem_attention(q, kv, batch, seq, mem_len, *, bq=512):
    T = batch * seq
    nq = seq // bq
    return pl.pallas_call(
        _mem_attn_kernel,
        grid=(batch, nq),
        in_specs=[
            pl.BlockSpec((bq, MEM_W), lambda b, i: (b * nq + i, 0)),
            pl.BlockSpec((mem_len, 2 * MEM_W), lambda b, i: (b, 0)),
        ],
        out_specs=pl.BlockSpec((bq, MEM_W), lambda b, i: (b * nq + i, 0)),
        out_shape=jax.ShapeDtypeStruct((T, MEM_W), BF16),
        compiler_params=_params(("arbitrary", "arbitrary")),
        name="mem_attention",
    )(q, kv)


def _extract_max(work, rows):
    m = jnp.max(work, axis=0, keepdims=True)
    first = jnp.min(jnp.where(work == m, rows, float(work.shape[0])), axis=0, keepdims=True)
    return m, rows == first


def _top16_rows(scores, dests, work_scr, rank_scr, base):
    shape = scores[0].shape
    rows = lax.broadcasted_iota(jnp.int32, shape, 0).astype(F32)
    chains = range(len(scores))

    def run(first_only):
        for n, s in enumerate(scores):
            work_scr[base + n] = s
            rank_scr[base + n] = jnp.full(shape, float(PEER_TOPK), F32)

        def body(a, carry):
            for n in chains:
                work = work_scr[base + n]
                if first_only:
                    m, hit = _extract_max(work, rows)
                else:
                    m = jnp.max(work, axis=0, keepdims=True)
                    hit = work == m
                ref, idx = dests[n]
                ref[idx, pl.ds(a, 1), :] = m
                work_scr[base + n] = jnp.where(hit, -jnp.inf, work)
                rank_scr[base + n] = jnp.where(hit, jnp.asarray(a, F32), rank_scr[base + n])
            return carry

        lax.fori_loop(0, PEER_TOPK, body, 0)

    run(first_only=False)
    excess = jnp.zeros((1, shape[1]), F32)
    for n in chains:
        taken = jnp.sum(jnp.where(rank_scr[base + n] < float(PEER_TOPK), 1.0, 0.0), axis=0, keepdims=True)
        excess = jnp.maximum(excess, jnp.abs(taken - float(PEER_TOPK)))
    pl.when(jnp.max(excess) > 0.0)(lambda: run(first_only=True))
    return [rank_scr[base + n] for n in chains]


def _route_kernel(q_ref, keys_ref, n1_ref, e1_ref, r2_ref, e2_ref, v1_scr, v2_scr, c_scr, sel_scr, work_scr, rank_scr,
                  *, tb, group):
    half = PEER_QDIM // 2
    dn = (((1,), (1,)), ((), ()))
    s1_all = lax.dot_general(keys_ref[0, 0], q_ref[:, :half], dn, preferred_element_type=F32)
    s2_all = lax.dot_general(keys_ref[0, 1], q_ref[:, half:], dn, preferred_element_type=F32)
    n_c = tb // LANES
    cols = [slice(c * LANES, (c + 1) * LANES) for c in range(n_c)]
    s1 = [s1_all[:, cl] for cl in cols]
    s2 = [s2_all[:, cl] for cl in cols]
    rank1, rank2 = [], []
    for c0 in range(0, n_c, group):
        cs = list(range(c0, c0 + group))
        ranks = _top16_rows([s1[c] for c in cs] + [s2[c] for c in cs],
                            [(v1_scr, c) for c in cs] + [(v2_scr, c) for c in cs], work_scr, rank_scr, 2 * c0)
        rank1 += ranks[:group]
        rank2 += ranks[group:]
    for c in range(n_c):
        for p, (a, b) in enumerate(_CELLS):
            c_scr[c, p:p + 1, :] = v1_scr[c, a:a + 1, :] + v2_scr[c, b:b + 1, :]
        if _CELL_ROWS > _N_CELLS:
            c_scr[c, _N_CELLS:, :] = jnp.full((_CELL_ROWS - _N_CELLS, LANES), -jnp.inf, F32)
    cand = [c_scr[c] for c in range(n_c)]
    prow = lax.broadcasted_iota(jnp.int32, cand[0].shape, 0).astype(F32)

    def body(t, carry):
        out = []
        for work, picked in carry:
            _, hit = _extract_max(work, prow)
            out.append((jnp.where(hit, -jnp.inf, work), jnp.where(hit, 1.0, picked)))
        return tuple(out)

    picked = [pk for _, pk in lax.fori_loop(0, PEER_TOPK, body,
                                            tuple((cd, jnp.zeros(cd.shape, F32)) for cd in cand))]
    for c in range(n_c):
        top1 = v1_scr[c, 0:1, :]
        top2 = v2_scr[c, 0:1, :]
        z = jnp.sum(jnp.where(picked[c] > 0, jnp.exp(cand[c] - (top1 + top2)), 0.0), axis=0, keepdims=True)
        sel_scr[c] = picked[c]
        n1 = jnp.zeros(s1[c].shape, F32)
        p0 = 0
        for a in range(PEER_TOPK):
            width = sum(1 for (aa, _) in _CELLS if aa == a)
            n_a = jnp.sum(sel_scr[c, p0:p0 + width, :], axis=0, keepdims=True)
            n1 = jnp.where(rank1[c] == float(a), n_a, n1)
            p0 += width
        n1_ref[0, :, cols[c]] = n1
        e1_ref[0, :, cols[c]] = jnp.exp(s1[c] - top1) / z
        r2_ref[0, :, cols[c]] = rank2[c]
        e2_ref[0, :, cols[c]] = jnp.exp(s2[c] - top2)


def _peer_route(q, sub_keys, *, tb=512, group=2):
    T = q.shape[0]
    shape = jax.ShapeDtypeStruct((PEER_HEADS, PEER_NKEYS, T), F32)
    out_spec = pl.BlockSpec((1, PEER_NKEYS, tb), lambda i, h: (h, 0, i))
    return pl.pallas_call(
        functools.partial(_route_kernel, tb=tb, group=group),
        grid=(T // tb, PEER_HEADS),
        in_specs=[
            pl.BlockSpec((tb, PEER_QDIM), lambda i, h: (i, h)),
            pl.BlockSpec((1, 2, PEER_NKEYS, PEER_QDIM // 2), lambda i, h: (h, 0, 0, 0)),
        ],
        out_specs=[out_spec] * 4,
        out_shape=[shape] * 4,
        scratch_shapes=[pltpu.VMEM((tb // LANES, PEER_TOPK, LANES), F32),
                        pltpu.VMEM((tb // LANES, PEER_TOPK, LANES), F32),
                        pltpu.VMEM((tb // LANES, _CELL_ROWS, LANES), F32),
                        pltpu.VMEM((tb // LANES, _CELL_ROWS, LANES), F32),
                        pltpu.VMEM((2 * (tb // LANES), PEER_NKEYS, LANES), F32),
                        pltpu.VMEM((2 * (tb // LANES), PEER_NKEYS, LANES), F32)],
        compiler_params=_params(("arbitrary", "arbitrary")),
        name="peer_route",
    )(q, sub_keys)


def _peer_kernel(xnt_ref, u_ref, vt_ref, n1_ref, e1_ref, r2_ref, e2_ref, o_ref, a_scr, w_scr, *, eb, tc):
    k = pl.program_id(1)

    @pl.when(k == 0)
    def _():
        o_ref[...] = jnp.zeros(o_ref.shape, F32)
        a_scr[...] = jnp.dot(u_ref[...], xnt_ref[...], preferred_element_type=F32)

    @pl.when(k > 0)
    def _():
        n_i = eb // PEER_NKEYS
        for c in range(a_scr.shape[1] // tc):
            cols = slice(c * tc, (c + 1) * tc)
            for a in range(n_i):
                rows = slice(a * PEER_NKEYS, (a + 1) * PEER_NKEYS)
                i_row = k * n_i + (a - n_i)
                a_t = a_scr[rows, cols]
                act = 0.5 * a_t * (1.0 + lax.erf(a_t * np.float32(np.sqrt(0.5))))
                gate = jnp.zeros((PEER_NKEYS, tc), F32)
                for h in range(PEER_HEADS):
                    cnt = n1_ref[h, pl.ds(i_row, 1), :][:, cols]
                    g1 = e1_ref[h, pl.ds(i_row, 1), :][:, cols]
                    gate = gate + jnp.where(r2_ref[h, :, cols] < cnt, e2_ref[h, :, cols] * g1, 0.0)
                w_scr[rows, cols] = (act * gate).astype(BF16)
        a_scr[...] = jnp.dot(u_ref[...], xnt_ref[...], preferred_element_type=F32)
        o_ref[...] += jnp.dot(vt_ref[0], w_scr[...], preferred_element_type=F32)


def _peer_mix(xnt, u, vt, n1, e1, r2, e2, *, tb=512, tc=128):
    D, T = xnt.shape
    E = u.shape[0]
    n_e, _, eb = vt.shape
    once = pl.Buffered(1)
    route_spec = pl.BlockSpec((PEER_HEADS, PEER_NKEYS, tb), lambda i, k: (0, 0, i), pipeline_mode=once)
    return pl.pallas_call(
        functools.partial(_peer_kernel, eb=eb, tc=tc),
        grid=(T // tb, n_e + 1),
        in_specs=[
            pl.BlockSpec((D, tb), lambda i, k: (0, i), pipeline_mode=once),
            pl.BlockSpec((eb, D), lambda i, k: (jnp.minimum(k, n_e - 1), 0)),
            pl.BlockSpec((1, D, eb), lambda i, k: (jnp.maximum(k - 1, 0), 0, 0)),
            route_spec, route_spec, route_spec, route_spec,
        ],
        out_specs=pl.BlockSpec((D, tb), lambda i, k: (0, i)),
        out_shape=jax.ShapeDtypeStruct((D, T), F32),
        scratch_shapes=[pltpu.VMEM((eb, tb), F32), pltpu.VMEM((eb, tb), BF16)],
        compiler_params=_params(("arbitrary", "arbitrary")),
        name="peer_mix",
    )(xnt, u, vt, n1, e1, r2, e2)


def _add_norm_kernel(a_ref, bt_ref, g_ref, o_ref, *, norm):
    x = a_ref[...] + bt_ref[...].T
    if norm:
        ms = jnp.mean(x * x, axis=-1, keepdims=True)
        x = (x * lax.rsqrt(ms + RMS_EPS)) * g_ref[...]
    o_ref[...] = x


def _add_norm(a, bt, gain, *, norm, bm=256):
    T, D = a.shape
    spec = pl.BlockSpec((bm, D), lambda i: (i, 0))
    return pl.pallas_call(
        functools.partial(_add_norm_kernel, norm=norm),
        grid=(T // bm,),
        in_specs=[spec, pl.BlockSpec((D, bm), lambda i: (0, i)), pl.BlockSpec((1, D), lambda i: (0, 0))],
        out_specs=spec,
        out_shape=jax.ShapeDtypeStruct((T, D), F32),
        compiler_params=_params(("arbitrary",)),
        name="add_norm",
    )(a, bt, gain.reshape(1, D).astype(F32))


def _rope_tables(seq):
    half = HEAD_DIM // 2
    inv = ROPE_THETA ** (-jnp.arange(half, dtype=F32) / half)
    ang = jnp.arange(seq, dtype=jnp.int32).astype(F32)[:, None] * inv[None, :]
    cos, sin = jnp.cos(ang), jnp.sin(ang)
    return jnp.concatenate([cos, cos], axis=-1), jnp.concatenate([-sin, sin], axis=-1)


def kernel(x, mem, norm_mix, w_in, swa_sink, nat_rel_bias, out_norm_swa, out_norm_nat, w_out, norm_mem_q,
           norm_mem_kv, w_mem_q, w_mem_k, w_mem_v, w_mem_o, norm_ffn, w_peer_q, peer_sub_keys, peer_u, peer_v,
           norm_final):
    B, S, D = x.shape
    M = mem.shape[1]
    T = B * S
    depth = w_in.shape[0]
    assert depth >= 1
    rope = _rope_tables(S)
    h = x.reshape(T, D)
    mem2 = mem.reshape(B * M, D)
    rope_cols = SWA_Q_W + SWA_KV_W
    for l in range(depth):
        qkv = _rms_matmul([h], [norm_mix[l]], w_in[l].astype(BF16), bm=512, bn=1024, out_dtype=BF16,
                          rope=rope, rope_cols=rope_cols, name="in_proj")
        oa = _swa_attention(qkv, swa_sink[l], B, S)
        ob = _nat_attention(qkv, nat_rel_bias[l], B, S)
        h = _rms_matmul([oa, ob], [out_norm_swa[l], out_norm_nat[l]], w_out[l].astype(BF16), bm=512, bn=1024,
                        out_dtype=F32, res=h, name="out_proj")
        w_kv = jnp.concatenate([w_mem_k[l], w_mem_v[l]], axis=1).astype(BF16)
        kv = _rms_matmul([mem2], [norm_mem_kv[l]], w_kv, bm=512, bn=512, out_dtype=BF16, name="mem_kv_proj")
        qm = _rms_matmul([h], [norm_mem_q[l]], w_mem_q[l].astype(BF16), bm=512, bn=MEM_W, out_dtype=BF16,
                         name="mem_q_proj")
        om = _mem_attention(qm, kv, B, S, M)
        h = _rms_matmul([om], None, w_mem_o[l].astype(BF16), bm=512, bn=D, out_dtype=F32, res=h,
                        name="mem_o_proj")
        qp, xn = _rms_matmul([h], [norm_ffn[l]], w_peer_q[l].astype(BF16), bm=512, bn=PEER_HEADS * PEER_QDIM,
                             out_dtype=BF16, emit_xn=True, name="peer_q_proj")
        n1, e1, r2, e2 = _peer_route(qp, peer_sub_keys[l].astype(BF16))
        eb = 512
        vt = jnp.transpose(peer_v[l].reshape(-1, eb, D), (0, 2, 1)).astype(BF16)
        peer = _peer_mix(xn.T, peer_u[l].astype(BF16), vt, n1, e1, r2, e2)
        last = l == depth - 1
        h = _add_norm(h, peer, norm_final if last else jnp.ones((D,), F32), norm=last)
    return h.reshape(B, S, D)
```

```python
import functools

import numpy as np
import jax
import jax.numpy as jnp
from jax import lax
from jax.experimental import pallas as pl
from jax.experimental.pallas import tpu as pltpu

F32 = jnp.float32
BF16 = jnp.bfloat16

D_MODEL = 4096
HEAD_DIM = 128
SWA_Q_HEADS = 16
SWA_KV_HEADS = 4
SWA_GROUP = SWA_Q_HEADS // SWA_KV_HEADS
SWA_WINDOW = 128
SWA_BLOCK = 128
ROPE_THETA = 10000.0
NAT_HEADS = 16
NAT_KR_MAX = 8
NAT_KC = 16
GRID_W = 64
NAT_QROWS = 2
SWA_Q_W = SWA_Q_HEADS * HEAD_DIM
SWA_KV_W = SWA_KV_HEADS * HEAD_DIM
NAT_W = NAT_HEADS * HEAD_DIM
MEM_HEADS = 4
MEM_HEAD_DIM = 256
MEM_W = MEM_HEADS * MEM_HEAD_DIM
PEER_HEADS = 8
PEER_NKEYS = 128
PEER_QDIM = 256
PEER_TOPK = 16
RMS_EPS = 1e-6
NEG_INF = -1e30

LANES = 128
VMEM_LIMIT_BYTES = 56 * 1024 * 1024

_CELLS = [(a, b) for a in range(PEER_TOPK) for b in range(PEER_TOPK) if (a + 1) * (b + 1) <= PEER_TOPK]
_N_CELLS = len(_CELLS)
_CELL_ROWS = -(-_N_CELLS // 8) * 8


def _params(sem, flags=None):
    return pltpu.CompilerParams(dimension_semantics=sem, vmem_limit_bytes=VMEM_LIMIT_BYTES, flags=flags)


def _rms_matmul_kernel(*refs, seg_widths, norm, has_res, rope_cols, emit_xn):
    n_seg = len(seg_widths)
    pos = 0
    x_refs = refs[pos:pos + n_seg]; pos += n_seg
    g_refs = ()
    if norm:
        g_refs = refs[pos:pos + n_seg]; pos += n_seg
    w_ref = refs[pos]; pos += 1
    res_ref = None
    if has_res:
        res_ref = refs[pos]; pos += 1
    cos_ref = sin_ref = None
    if rope_cols:
        cos_ref, sin_ref = refs[pos], refs[pos + 1]; pos += 2
    o_ref = refs[pos]; pos += 1
    xn_out_ref = None
    if emit_xn:
        xn_out_ref = refs[pos]; pos += 1
    xn_ref = refs[pos]

    j = pl.program_id(1)

    @pl.when(j == 0)
    def _():
        off = 0
        for s in range(n_seg):
            x = x_refs[s][...].astype(F32)
            if norm:
                ms = jnp.mean(x * x, axis=-1, keepdims=True)
                x = (x * lax.rsqrt(ms + RMS_EPS)) * g_refs[s][...]
            xn_ref[:, off:off + seg_widths[s]] = x.astype(BF16)
            off += seg_widths[s]
        if emit_xn:
            xn_out_ref[...] = xn_ref[...]

    acc = jnp.dot(xn_ref[...], w_ref[...], preferred_element_type=F32)
    if has_res:
        acc = acc + res_ref[...]

    bn = acc.shape[1]
    if rope_cols:
        def store(n_roped_heads):
            for g in range(n_roped_heads):
                sl = slice(g * HEAD_DIM, (g + 1) * HEAD_DIM)
                rot = pltpu.roll(acc[:, sl], HEAD_DIM // 2, 1)
                o_ref[:, sl] = (acc[:, sl] * cos_ref[...] + rot * sin_ref[...]).astype(o_ref.dtype)
            if n_roped_heads * HEAD_DIM < bn:
                o_ref[:, n_roped_heads * HEAD_DIM:] = acc[:, n_roped_heads * HEAD_DIM:].astype(o_ref.dtype)

        full_blocks, rem = divmod(rope_cols, bn)
        pl.when(j < full_blocks)(lambda: store(bn // HEAD_DIM))
        if rem:
            pl.when(j == full_blocks)(lambda: store(rem // HEAD_DIM))
        pl.when(j >= full_blocks + (1 if rem else 0))(lambda: store(0))
    else:
        o_ref[...] = acc.astype(o_ref.dtype)


def _rms_matmul(xs, gains, w, *, bm, bn, out_dtype, res=None, rope=None, rope_cols=0, emit_xn=False, name):
    T = xs[0].shape[0]
    seg_widths = tuple(int(x.shape[1]) for x in xs)
    K = sum(seg_widths)
    N = w.shape[1]
    norm = gains is not None
    assert w.shape[0] == K and T % bm == 0 and N % bn == 0 and rope_cols % HEAD_DIM == 0
    in_specs = [pl.BlockSpec((bm, kw), lambda i, j: (i, 0)) for kw in seg_widths]
    args = list(xs)
    if norm:
        in_specs += [pl.BlockSpec((1, kw), lambda i, j: (0, 0)) for kw in seg_widths]
        args += [g.reshape(1, -1).astype(F32) for g in gains]
    in_specs.append(pl.BlockSpec((K, bn), lambda i, j: (0, j), pipeline_mode=pl.Buffered(1) if bn == N else None))
    args.append(w)
    if res is not None:
        in_specs.append(pl.BlockSpec((bm, bn), lambda i, j: (i, j)))
        args.append(res)
    if rope_cols:
        cos, sin = rope
        pos_blocks = cos.shape[0] // bm
        in_specs += [pl.BlockSpec((bm, HEAD_DIM), lambda i, j: (i % pos_blocks, 0))] * 2
        args += [cos, sin]
    out_shape = [jax.ShapeDtypeStruct((T, N), out_dtype)]
    out_specs = [pl.BlockSpec((bm, bn), lambda i, j: (i, j))]
    if emit_xn:
        out_shape.append(jax.ShapeDtypeStruct((T, K), BF16))
        out_specs.append(pl.BlockSpec((bm, K), lambda i, j: (i, 0)))
    kern = functools.partial(_rms_matmul_kernel, seg_widths=seg_widths, norm=norm, has_res=res is not None,
                             rope_cols=rope_cols, emit_xn=emit_xn)
    outs = pl.pallas_call(
        kern,
        grid=(T // bm, N // bn),
        in_specs=in_specs,
        out_specs=out_specs,
        out_shape=out_shape,
        scratch_shapes=[pltpu.VMEM((bm, K), BF16)],
        compiler_params=_params(("arbitrary", "arbitrary")),
        name=name,
    )(*args)
    return outs if emit_xn else outs[0]


def _swa_kernel(q_ref, k_ref, v_ref, sink_ref, o_ref, *, seq, qblocks):
    band = 3 * SWA_BLOCK
    scale = HEAD_DIM ** -0.5
    sink = jnp.concatenate([jnp.broadcast_to(sink_ref[0, g:g + 1, 0:1], (SWA_BLOCK, 1))
                            for g in range(SWA_GROUP)], axis=0)
    blocks = range(qblocks)
    rows = [slice(t * SWA_BLOCK, (t + 1) * SWA_BLOCK) for t in blocks]
    scores, values = [], []
    for t in blocks:
        n = pl.program_id(2) * qblocks + t
        start = pl.multiple_of(jnp.clip((n - 1) * SWA_BLOCK, 0, seq - band), SWA_BLOCK)
        k = k_ref[pl.ds(start, band), :]
        values.append(v_ref[pl.ds(start, band), :])
        qpos = n * SWA_BLOCK + lax.broadcasted_iota(jnp.int32, (SWA_BLOCK, band), 0)
        kpos = start + lax.broadcasted_iota(jnp.int32, (SWA_BLOCK, band), 1)
        valid = jnp.abs(qpos - kpos) <= SWA_WINDOW
        q = jnp.concatenate([q_ref[rows[t], g * HEAD_DIM:(g + 1) * HEAD_DIM] for g in range(SWA_GROUP)], axis=0)
        s = lax.dot_general(q, k, (((1,), (1,)), ((), ())), preferred_element_type=F32) * scale
        scores.append(jnp.where(jnp.concatenate([valid] * SWA_GROUP, axis=0), s, NEG_INF))
    maxes = [jnp.maximum(jnp.max(s, axis=-1, keepdims=True), sink) for s in scores]
    probs = [jnp.exp(s - m) for s, m in zip(scores, maxes)]
    denoms = [jnp.sum(p, axis=-1, keepdims=True) + jnp.exp(sink - m) for p, m in zip(probs, maxes)]
    outs = [jnp.dot((p * (1.0 / d)).astype(BF16), v, preferred_element_type=F32)
            for p, d, v in zip(probs, denoms, values)]
    for t in blocks:
        for g in range(SWA_GROUP):
            o_ref[rows[t], g * HEAD_DIM:(g + 1) * HEAD_DIM] = outs[t][g * SWA_BLOCK:(g + 1) * SWA_BLOCK, :]


def _swa_attention(qkv, sink, batch, seq, *, qblocks=4):
    T = batch * seq
    nb = seq // (SWA_BLOCK * qblocks)
    gw = SWA_GROUP * HEAD_DIM
    k_col0 = SWA_Q_W // HEAD_DIM
    v_col0 = (SWA_Q_W + SWA_KV_W) // HEAD_DIM
    sink_b = jnp.broadcast_to(sink.astype(F32).reshape(SWA_KV_HEADS, SWA_GROUP, 1),
                              (SWA_KV_HEADS, SWA_GROUP, LANES))
    return pl.pallas_call(
        functools.partial(_swa_kernel, seq=seq, qblocks=qblocks),
        grid=(batch, SWA_KV_HEADS, nb),
        in_specs=[
            pl.BlockSpec((SWA_BLOCK * qblocks, gw), lambda b, h, n: (b * nb + n, h)),
            pl.BlockSpec((seq, HEAD_DIM), lambda b, h, n: (b, k_col0 + h)),
            pl.BlockSpec((seq, HEAD_DIM), lambda b, h, n: (b, v_col0 + h)),
            pl.BlockSpec((1, SWA_GROUP, LANES), lambda b, h, n: (h, 0, 0)),
        ],
        out_specs=pl.BlockSpec((SWA_BLOCK * qblocks, gw), lambda b, h, n: (b * nb + n, h)),
        out_shape=jax.ShapeDtypeStruct((T, SWA_Q_W), F32),
        compiler_params=_params(("arbitrary", "arbitrary", "arbitrary")),
        name="swa_attention",
    )(qkv, qkv, qkv, sink_b)


def _nat_tables(seq):
    rows = seq // GRID_W
    kr = min(NAT_KR_MAX, rows)
    span = min(rows, NAT_QROWS + kr - 1)
    nb = rows // NAT_QROWS
    qbl = NAT_QROWS * GRID_W
    r0 = np.arange(nb) * NAT_QROWS
    kstart = np.clip(r0 - kr // 2, 0, rows - span)
    key_rows = kstart[:, None] + np.arange(span)[None, :]
    q_row = r0[:, None] + (np.arange(qbl) // GRID_W)[None, :]
    q_col = np.broadcast_to((np.arange(qbl) % GRID_W)[None, :], q_row.shape)
    k_row = np.repeat(key_rows, GRID_W, axis=1)
    k_col = np.broadcast_to(np.tile(np.arange(GRID_W), span)[None, :], k_row.shape)
    rs = np.clip(q_row - kr // 2, 0, rows - kr)[:, :, None]
    cs = np.clip(q_col - NAT_KC // 2, 0, GRID_W - NAT_KC)[:, :, None]
    kr3, kc3 = k_row[:, None, :], k_col[:, None, :]
    valid = (kr3 >= rs) & (kr3 < rs + kr) & (kc3 >= cs) & (kc3 < cs + NAT_KC)
    dr = np.clip(kr3 - q_row[:, :, None] + NAT_KR_MAX - 1, 0, 2 * NAT_KR_MAX - 2)
    dc = np.clip(kc3 - q_col[:, :, None] + NAT_KC - 1, 0, 2 * NAT_KC - 2)
    flat = np.where(valid, dr * (2 * NAT_KC - 1) + dc, -1).reshape(nb, -1)
    _, first, inverse = np.unique(flat, axis=0, return_index=True, return_inverse=True)
    valid, dr, dc = valid[first], dr[first], dc[first]
    dr_tab = dr[:, ::GRID_W, ::GRID_W]
    assert (dr == np.repeat(np.repeat(dr_tab, GRID_W, axis=1), GRID_W, axis=2)).all()
    col = np.arange(GRID_W)
    toeplitz = np.clip(col[None, :] - col[:, None] + NAT_KC - 1, 0, 2 * NAT_KC - 2)
    assert (dc == np.tile(toeplitz, (NAT_QROWS, span))[None]).all()
    far = np.tile(np.abs(col[None, :] - col[:, None]) >= NAT_KC, (NAT_QROWS, span))
    assert not (valid & far[None]).any()
    return dict(nb=nb, qbl=qbl, span=span, nkeys=span * GRID_W, start=(kstart * GRID_W).astype(np.int32),
                pid=np.asarray(inverse).reshape(-1).astype(np.int32), valid=valid, dr_tab=dr_tab)


def _nat_bias_kernel(rows_ref, valid_ref, o_ref, *, span):
    n_pairs = -(-span // 2)
    for qr in range(NAT_QROWS):
        rs = slice(qr * GRID_W, (qr + 1) * GRID_W)
        for pair in range(n_pairs):
            row = rows_ref[0, 0, qr * n_pairs + pair:qr * n_pairs + pair + 1, :]
            blk = pltpu.roll(jnp.broadcast_to(row, (GRID_W, LANES)), LANES - (NAT_KC - 1), 1,
                             stride=1, stride_axis=0)
            width = min(LANES, span * GRID_W - pair * LANES)
            cs = slice(pair * LANES, pair * LANES + width)
            o_ref[0, 0, rs, cs] = jnp.where(valid_ref[0, rs, cs] > 0, blk[:, :width], NEG_INF)


def _nat_bias_table(rel_bias, tab):
    span, qbl, nkeys = tab["span"], tab["qbl"], tab["nkeys"]
    n_pat = tab["valid"].shape[0]
    n_pairs = -(-span // 2)
    nrel = 2 * NAT_KC - 1
    assert 2 * GRID_W == LANES and nrel <= GRID_W
    rows = rel_bias.astype(F32)[:, tab["dr_tab"]]
    rows = jnp.pad(rows, ((0, 0), (0, 0), (0, 0), (0, 2 * n_pairs - span), (0, GRID_W - nrel)))
    rows = jnp.transpose(rows.reshape(NAT_HEADS, n_pat, NAT_QROWS * n_pairs, LANES), (1, 0, 2, 3))
    valid = jnp.asarray(tab["valid"].astype(np.float32))
    return pl.pallas_call(
        functools.partial(_nat_bias_kernel, span=span),
        grid=(n_pat, NAT_HEADS),
        in_specs=[
            pl.BlockSpec((1, 1, NAT_QROWS * n_pairs, LANES), lambda p, h: (p, h, 0, 0)),
            pl.BlockSpec((1, qbl, nkeys), lambda p, h: (p, 0, 0)),
        ],
        out_specs=pl.BlockSpec((1, 1, qbl, nkeys), lambda p, h: (p, h, 0, 0)),
        out_shape=jax.ShapeDtypeStruct((n_pat, NAT_HEADS, qbl, nkeys), F32),
        compiler_params=_params(("arbitrary", "arbitrary")),
        name="nat_bias_table",
    )(rows, valid)


def _nat_kernel(pid_ref, start_ref, q_ref, k_ref, v_ref, *rest, nkeys, heads, qbl):
    del pid_ref
    bias_refs, o_ref = rest[:-1], rest[-1]
    scale = HEAD_DIM ** -0.5
    sls = [slice(h * HEAD_DIM, (h + 1) * HEAD_DIM) for h in range(heads)]
    blocks = range(len(bias_refs))
    rows = [slice(t * qbl, (t + 1) * qbl) for t in blocks]
    scores, values = [], []
    for t in blocks:
        start = pl.multiple_of(start_ref[pl.program_id(2) * len(bias_refs) + t], GRID_W)
        q = jnp.stack([q_ref[rows[t], sl] for sl in sls], axis=0)
        k = jnp.stack([k_ref[pl.ds(start, nkeys), sl] for sl in sls], axis=0)
        values.append(jnp.stack([v_ref[pl.ds(start, nkeys), sl] for sl in sls], axis=0))
        s = lax.dot_general(q, k, (((2,), (2,)), ((0,), (0,))), preferred_element_type=F32) * scale
        scores.append(s + bias_refs[t][0])
    maxes = [jnp.max(s, axis=-1, keepdims=True) for s in scores]
    probs = [jnp.exp(s - m) for s, m in zip(scores, maxes)]
    probs = [(p * (1.0 / jnp.sum(p, axis=-1, keepdims=True))).astype(BF16) for p in probs]
    outs = [lax.dot_general(p, v, (((2,), (1,)), ((0,), (0,))), preferred_element_type=F32)
            for p, v in zip(probs, values)]
    for t in blocks:
        for h, sl in enumerate(sls):
            o_ref[rows[t], sl] = outs[t][h]


def _nat_attention(qkv, rel_bias, batch, seq, *, qblocks=4):
    T = batch * seq
    tab = _nat_tables(seq)
    qbl, nkeys = tab["qbl"], tab["nkeys"]
    nb = tab["nb"] // qblocks
    bias = _nat_bias_table(rel_bias, tab)
    hpg = 4
    gw = hpg * HEAD_DIM
    n_hg = NAT_HEADS // hpg
    q0 = (SWA_Q_W + 2 * SWA_KV_W) // gw
    k0 = q0 + n_hg
    v0 = k0 + n_hg

    def bias_spec(t):
        return pl.BlockSpec((1, hpg, qbl, nkeys), lambda b, g, n, pid, st: (pid[n * qblocks + t], g, 0, 0))

    grid_spec = pltpu.PrefetchScalarGridSpec(
        num_scalar_prefetch=2,
        grid=(batch, n_hg, nb),
        in_specs=[
            pl.BlockSpec((qbl * qblocks, gw), lambda b, g, n, pid, st: (b * nb + n, q0 + g)),
            pl.BlockSpec((seq, gw), lambda b, g, n, pid, st: (b, k0 + g)),
            pl.BlockSpec((seq, gw), lambda b, g, n, pid, st: (b, v0 + g)),
        ] + [bias_spec(t) for t in range(qblocks)],
        out_specs=pl.BlockSpec((qbl * qblocks, gw), lambda b, g, n, pid, st: (b * nb + n, g)),
    )
    return pl.pallas_call(
        functools.partial(_nat_kernel, nkeys=nkeys, heads=hpg, qbl=qbl),
        grid_spec=grid_spec,
        out_shape=jax.ShapeDtypeStruct((T, NAT_W), F32),
        compiler_params=_params(("arbitrary", "arbitrary", "arbitrary")),
        name="nat_attention",
    )(jnp.asarray(tab["pid"]), jnp.asarray(tab["start"]), qkv, qkv, qkv, *([bias] * qblocks))


def _mem_attn_kernel(q_ref, kv_ref, o_ref):
    scale = MEM_HEAD_DIM ** -0.5
    for h in range(MEM_HEADS):
        sl = slice(h * MEM_HEAD_DIM, (h + 1) * MEM_HEAD_DIM)
        q = q_ref[:, sl]
        k = kv_ref[:, sl]
        v = kv_ref[:, MEM_W + h * MEM_HEAD_DIM:MEM_W + (h + 1) * MEM_HEAD_DIM]
        s = lax.dot_general(q, k, (((1,), (1,)), ((), ())), preferred_element_type=F32) * scale
        m = jnp.max(s, axis=-1, keepdims=True)
        p = jnp.exp(s - m)
        p = (p / jnp.sum(p, axis=-1, keepdims=True)).astype(BF16)
        o_ref[:, sl] = jnp.dot(p, v, preferred_element_type=F32).astype(o_ref.dtype)


def _mem_attention(q, kv, batch, seq, mem_len, *, bq=512):
    T = batch * seq
    nq = seq // bq
    return pl.pallas_call(
        _mem_attn_kernel,
        grid=(batch, nq),
        in_specs=[
            pl.BlockSpec((bq, MEM_W), lambda b, i: (b * nq + i, 0)),
            pl.BlockSpec((mem_len, 2 * MEM_W), lambda b, i: (b, 0)),
        ],
        out_specs=pl.BlockSpec((bq, MEM_W), lambda b, i: (b * nq + i, 0)),
        out_shape=jax.ShapeDtypeStruct((T, MEM_W), BF16),
        compiler_params=_params(("arbitrary", "arbitrary")),
        name="mem_attention",
    )(q, kv)


def _extract_max(work, rows):
    m = jnp.max(work, axis=0, keepdims=True)
    first = jnp.min(jnp.where(work == m, rows, float(work.shape[0])), axis=0, keepdims=True)
    return m, rows == first


def _top16_rows(scores, dests, work_scr, rank_scr, base):
    shape = scores[0].shape
    rows = lax.broadcasted_iota(jnp.int32, shape, 0).astype(F32)
    chains = range(len(scores))

    def run(first_only):
        for n, s in enumerate(scores):
            work_scr[base + n] = s
            rank_scr[base + n] = jnp.full(shape, float(PEER_TOPK), F32)

        def body(a, carry):
            for n in chains:
                work = work_scr[base + n]
                if first_only:
                    m, hit = _extract_max(work, rows)
                else:
                    m = jnp.max(work, axis=0, keepdims=True)
                    hit = work == m
                ref, idx = dests[n]
                ref[idx, pl.ds(a, 1), :] = m
                work_scr[base + n] = jnp.where(hit, -jnp.inf, work)
                rank_scr[base + n] = jnp.where(hit, jnp.asarray(a, F32), rank_scr[base + n])
            return carry

        lax.fori_loop(0, PEER_TOPK, body, 0)

    run(first_only=False)
    excess = jnp.zeros((1, shape[1]), F32)
    for n in chains:
        taken = jnp.sum(jnp.where(rank_scr[base + n] < float(PEER_TOPK), 1.0, 0.0), axis=0, keepdims=True)
        excess = jnp.maximum(excess, jnp.abs(taken - float(PEER_TOPK)))
    pl.when(jnp.max(excess) > 0.0)(lambda: run(first_only=True))
    return [rank_scr[base + n] for n in chains]


def _route_kernel(q_ref, keys_ref, n1_ref, e1_ref, r2_ref, e2_ref, v1_scr, v2_scr, c_scr, sel_scr, work_scr, rank_scr,
                  *, tb, group):
    half = PEER_QDIM // 2
    dn = (((1,), (1,)), ((), ()))
    s1_all = lax.dot_general(keys_ref[0, 0], q_ref[:, :half], dn, preferred_element_type=F32)
    s2_all = lax.dot_general(keys_ref[0, 1], q_ref[:, half:], dn, preferred_element_type=F32)
    n_c = tb // LANES
    cols = [slice(c * LANES, (c + 1) * LANES) for c in range(n_c)]
    s1 = [s1_all[:, cl] for cl in cols]
    s2 = [s2_all[:, cl] for cl in cols]
    rank1, rank2 = [], []
    for c0 in range(0, n_c, group):
        cs = list(range(c0, c0 + group))
        ranks = _top16_rows([s1[c] for c in cs] + [s2[c] for c in cs],
                            [(v1_scr, c) for c in cs] + [(v2_scr, c) for c in cs], work_scr, rank_scr, 2 * c0)
        rank1 += ranks[:group]
        rank2 += ranks[group:]
    for c in range(n_c):
        for p, (a, b) in enumerate(_CELLS):
            c_scr[c, p:p + 1, :] = v1_scr[c, a:a + 1, :] + v2_scr[c, b:b + 1, :]
        if _CELL_ROWS > _N_CELLS:
            c_scr[c, _N_CELLS:, :] = jnp.full((_CELL_ROWS - _N_CELLS, LANES), -jnp.inf, F32)
    cand = [c_scr[c] for c in range(n_c)]
    prow = lax.broadcasted_iota(jnp.int32, cand[0].shape, 0).astype(F32)

    def body(t, carry):
        out = []
        for work, picked in carry:
            _, hit = _extract_max(work, prow)
            out.append((jnp.where(hit, -jnp.inf, work), jnp.where(hit, 1.0, picked)))
        return tuple(out)

    picked = [pk for _, pk in lax.fori_loop(0, PEER_TOPK, body,
                                            tuple((cd, jnp.zeros(cd.shape, F32)) for cd in cand))]
    for c in range(n_c):
        top1 = v1_scr[c, 0:1, :]
        top2 = v2_scr[c, 0:1, :]
        z = jnp.sum(jnp.where(picked[c] > 0, jnp.exp(cand[c] - (top1 + top2)), 0.0), axis=0, keepdims=True)
        sel_scr[c] = picked[c]
        n1 = jnp.zeros(s1[c].shape, F32)
        p0 = 0
        for a in range(PEER_TOPK):
            width = sum(1 for (aa, _) in _CELLS if aa == a)
            n_a = jnp.sum(sel_scr[c, p0:p0 + width, :], axis=0, keepdims=True)
            n1 = jnp.where(rank1[c] == float(a), n_a, n1)
            p0 += width
        n1_ref[0, :, cols[c]] = n1
        e1_ref[0, :, cols[c]] = jnp.exp(s1[c] - top1) / z
        r2_ref[0, :, cols[c]] = rank2[c].astype(r2_ref.dtype)
        e2_ref[0, :, cols[c]] = jnp.exp(s2[c] - top2).astype(e2_ref.dtype)


def _peer_route(q, sub_keys, *, tb=512, group=2):
    T = q.shape[0]
    shape = jax.ShapeDtypeStruct((PEER_HEADS, PEER_NKEYS, T), F32)
    shape_packed = jax.ShapeDtypeStruct((PEER_HEADS, PEER_NKEYS, T), BF16)
    out_spec = pl.BlockSpec((1, PEER_NKEYS, tb), lambda i, h: (h, 0, i))
    return pl.pallas_call(
        functools.partial(_route_kernel, tb=tb, group=group),
        grid=(T // tb, PEER_HEADS),
        in_specs=[
            pl.BlockSpec((tb, PEER_QDIM), lambda i, h: (i, h)),
            pl.BlockSpec((1, 2, PEER_NKEYS, PEER_QDIM // 2), lambda i, h: (h, 0, 0, 0)),
        ],
        out_specs=[out_spec] * 4,
        out_shape=[shape, shape, shape_packed, shape_packed],
        scratch_shapes=[pltpu.VMEM((tb // LANES, PEER_TOPK, LANES), F32),
                        pltpu.VMEM((tb // LANES, PEER_TOPK, LANES), F32),
                        pltpu.VMEM((tb // LANES, _CELL_ROWS, LANES), F32),
                        pltpu.VMEM((tb // LANES, _CELL_ROWS, LANES), F32),
                        pltpu.VMEM((2 * (tb // LANES), PEER_NKEYS, LANES), F32),
                        pltpu.VMEM((2 * (tb // LANES), PEER_NKEYS, LANES), F32)],
        compiler_params=_params(("arbitrary", "arbitrary")),
        name="peer_route",
    )(q, sub_keys)


def _peer_kernel(xnt_ref, u_ref, vt_ref, n1_ref, e1_ref, r2_ref, e2_ref, o_ref, a_scr, w_scr, *, eb, tc):
    k = pl.program_id(1)

    @pl.when(k == 0)
    def _():
        o_ref[...] = jnp.zeros(o_ref.shape, F32)
        a_scr[...] = jnp.dot(u_ref[...], xnt_ref[...], preferred_element_type=F32)

    @pl.when(k > 0)
    def _():
        n_i = eb // PEER_NKEYS
        pack = 16
        groups = PEER_NKEYS // pack
        for c in range(a_scr.shape[1] // tc):
            cols = slice(c * tc, (c + 1) * tc)
            for a in range(n_i):
                rows = slice(a * PEER_NKEYS, (a + 1) * PEER_NKEYS)
                i_row = k * n_i + (a - n_i)
                a_t = a_scr[rows, cols]
                act = 0.5 * a_t * (1.0 + lax.erf(a_t * np.float32(np.sqrt(0.5))))
                gate = jnp.zeros((PEER_NKEYS, tc), BF16)
                for h in range(PEER_HEADS):
                    cnt = jnp.broadcast_to(n1_ref[h, pl.ds(i_row, 1), :][:, cols], (pack, tc)).astype(BF16)
                    g1 = jnp.broadcast_to(e1_ref[h, pl.ds(i_row, 1), :][:, cols], (pack, tc)).astype(BF16)
                    cnt = jnp.concatenate([cnt] * groups, axis=0)
                    g1 = jnp.concatenate([g1] * groups, axis=0)
                    gate = gate + jnp.where(r2_ref[h, :, cols] < cnt, e2_ref[h, :, cols] * g1,
                                            jnp.zeros((), BF16))
                w_scr[rows, cols] = (act * gate.astype(F32)).astype(BF16)
        a_scr[...] = jnp.dot(u_ref[...], xnt_ref[...], preferred_element_type=F32)
        o_ref[...] += jnp.dot(vt_ref[0], w_scr[...], preferred_element_type=F32)


def _peer_mix(xnt, u, vt, n1, e1, r2, e2, *, tb=512, tc=512):
    D, T = xnt.shape
    E = u.shape[0]
    n_e, _, eb = vt.shape
    once = pl.Buffered(1)
    route_spec = pl.BlockSpec((PEER_HEADS, PEER_NKEYS, tb), lambda i, k: (0, 0, i), pipeline_mode=once)
    return pl.pallas_call(
        functools.partial(_peer_kernel, eb=eb, tc=tc),
        grid=(T // tb, n_e + 1),
        in_specs=[
            pl.BlockSpec((D, tb), lambda i, k: (0, i), pipeline_mode=once),
            pl.BlockSpec((eb, D), lambda i, k: (jnp.minimum(k, n_e - 1), 0)),
            pl.BlockSpec((1, D, eb), lambda i, k: (jnp.maximum(k - 1, 0), 0, 0)),
            route_spec, route_spec, route_spec, route_spec,
        ],
        out_specs=pl.BlockSpec((D, tb), lambda i, k: (0, i)),
        out_shape=jax.ShapeDtypeStruct((D, T), F32),
        scratch_shapes=[pltpu.VMEM((eb, tb), F32), pltpu.VMEM((eb, tb), BF16)],
        compiler_params=_params(("arbitrary", "arbitrary")),
        name="peer_mix",
    )(xnt, u, vt, n1, e1, r2, e2)


def _add_norm_kernel(a_ref, bt_ref, g_ref, o_ref, *, norm):
    x = a_ref[...] + bt_ref[...].T
    if norm:
        ms = jnp.mean(x * x, axis=-1, keepdims=True)
        x = (x * lax.rsqrt(ms + RMS_EPS)) * g_ref[...]
    o_ref[...] = x


def _add_norm(a, bt, gain, *, norm, bm=256):
    T, D = a.shape
    spec = pl.BlockSpec((bm, D), lambda i: (i, 0))
    return pl.pallas_call(
        functools.partial(_add_norm_kernel, norm=norm),
        grid=(T // bm,),
        in_specs=[spec, pl.BlockSpec((D, bm), lambda i: (0, i)), pl.BlockSpec((1, D), lambda i: (0, 0))],
        out_specs=spec,
        out_shape=jax.ShapeDtypeStruct((T, D), F32),
        compiler_params=_params(("arbitrary",)),
        name="add_norm",
    )(a, bt, gain.reshape(1, D).astype(F32))


def _rope_tables(seq):
    half = HEAD_DIM // 2
    inv = ROPE_THETA ** (-jnp.arange(half, dtype=F32) / half)
    ang = jnp.arange(seq, dtype=jnp.int32).astype(F32)[:, None] * inv[None, :]
    cos, sin = jnp.cos(ang), jnp.sin(ang)
    return jnp.concatenate([cos, cos], axis=-1), jnp.concatenate([-sin, sin], axis=-1)


def kernel(x, mem, norm_mix, w_in, swa_sink, nat_rel_bias, out_norm_swa, out_norm_nat, w_out, norm_mem_q,
           norm_mem_kv, w_mem_q, w_mem_k, w_mem_v, w_mem_o, norm_ffn, w_peer_q, peer_sub_keys, peer_u, peer_v,
           norm_final):
    B, S, D = x.shape
    M = mem.shape[1]
    T = B * S
    depth = w_in.shape[0]
    assert depth >= 1
    rope = _rope_tables(S)
    h = x.reshape(T, D)
    mem2 = mem.reshape(B * M, D)
    rope_cols = SWA_Q_W + SWA_KV_W
    for l in range(depth):
        qkv = _rms_matmul([h], [norm_mix[l]], w_in[l].astype(BF16), bm=512, bn=1024, out_dtype=BF16,
                          rope=rope, rope_cols=rope_cols, name="in_proj")
        oa = _swa_attention(qkv, swa_sink[l], B, S)
        ob = _nat_attention(qkv, nat_rel_bias[l], B, S)
        h = _rms_matmul([oa, ob], [out_norm_swa[l], out_norm_nat[l]], w_out[l].astype(BF16), bm=512, bn=1024,
                        out_dtype=F32, res=h, name="out_proj")
        w_kv = jnp.concatenate([w_mem_k[l], w_mem_v[l]], axis=1).astype(BF16)
        kv = _rms_matmul([mem2], [norm_mem_kv[l]], w_kv, bm=512, bn=512, out_dtype=BF16, name="mem_kv_proj")
        qm = _rms_matmul([h], [norm_mem_q[l]], w_mem_q[l].astype(BF16), bm=512, bn=MEM_W, out_dtype=BF16,
                         name="mem_q_proj")
        om = _mem_attention(qm, kv, B, S, M)
        h = _rms_matmul([om], None, w_mem_o[l].astype(BF16), bm=512, bn=D, out_dtype=F32, res=h,
                        name="mem_o_proj")
        qp, xn = _rms_matmul([h], [norm_ffn[l]], w_peer_q[l].astype(BF16), bm=512, bn=PEER_HEADS * PEER_QDIM,
                             out_dtype=BF16, emit_xn=True, name="peer_q_proj")
        n1, e1, r2, e2 = _peer_route(qp, peer_sub_keys[l].astype(BF16))
        eb = 512
        vt = jnp.transpose(peer_v[l].reshape(-1, eb, D), (0, 2, 1)).astype(BF16)
        peer = _peer_mix(xn.T, peer_u[l].astype(BF16), vt, n1, e1, r2, e2)
        last = l == depth - 1
        h = _add_norm(h, peer, norm_final if last else jnp.ones((D,), F32), norm=last)
    return h.reshape(B, S, D)
```

```python
import functools

import numpy as np
import jax
import jax.numpy as jnp
from jax import lax
from jax.experimental import pallas as pl
from jax.experimental.pallas import tpu as pltpu

F32 = jnp.float32
BF16 = jnp.bfloat16

D_MODEL = 4096
HEAD_DIM = 128
SWA_Q_HEADS = 16
SWA_KV_HEADS = 4
SWA_GROUP = SWA_Q_HEADS // SWA_KV_HEADS
SWA_WINDOW = 128
SWA_BLOCK = 128
ROPE_THETA = 10000.0
NAT_HEADS = 16
NAT_KR_MAX = 8
NAT_KC = 16
GRID_W = 64
NAT_QROWS = 2
SWA_Q_W = SWA_Q_HEADS * HEAD_DIM
SWA_KV_W = SWA_KV_HEADS * HEAD_DIM
NAT_W = NAT_HEADS * HEAD_DIM
MEM_HEADS = 4
MEM_HEAD_DIM = 256
MEM_W = MEM_HEADS * MEM_HEAD_DIM
PEER_HEADS = 8
PEER_NKEYS = 128
PEER_QDIM = 256
PEER_TOPK = 16
RMS_EPS = 1e-6
NEG_INF = -1e30

LANES = 128
VMEM_LIMIT_BYTES = 56 * 1024 * 1024

_CELLS = [(a, b) for a in range(PEER_TOPK) for b in range(PEER_TOPK) if (a + 1) * (b + 1) <= PEER_TOPK]
_N_CELLS = len(_CELLS)
_CELL_ROWS = -(-_N_CELLS // 8) * 8


def _params(sem, vmem_limit_bytes=None):
    return pltpu.CompilerParams(dimension_semantics=sem, vmem_limit_bytes=vmem_limit_bytes or VMEM_LIMIT_BYTES)


def _rms_matmul_kernel(*refs, seg_widths, norm, has_res, rope_cols, emit_xn):
    n_seg = len(seg_widths)
    pos = 0
    x_refs = refs[pos:pos + n_seg]; pos += n_seg
    g_refs = ()
    if norm:
        g_refs = refs[pos:pos + n_seg]; pos += n_seg
    w_ref = refs[pos]; pos += 1
    res_ref = None
    if has_res:
        res_ref = refs[pos]; pos += 1
    cos_ref = sin_ref = None
    if rope_cols:
        cos_ref, sin_ref = refs[pos], refs[pos + 1]; pos += 2
    o_ref = refs[pos]; pos += 1
    xn_out_ref = None
    if emit_xn:
        xn_out_ref = refs[pos]; pos += 1
    xn_ref = refs[pos]

    j = pl.program_id(1)

    @pl.when(j == 0)
    def _():
        off = 0
        for s in range(n_seg):
            x = x_refs[s][...].astype(F32)
            if norm:
                ms = jnp.mean(x * x, axis=-1, keepdims=True)
                x = (x * lax.rsqrt(ms + RMS_EPS)) * g_refs[s][...]
            xn_ref[:, off:off + seg_widths[s]] = x.astype(BF16)
            off += seg_widths[s]
        if emit_xn:
            xn_out_ref[...] = xn_ref[...]

    acc = jnp.dot(xn_ref[...], w_ref[...], preferred_element_type=F32)
    if has_res:
        acc = acc + res_ref[...]

    bn = acc.shape[1]
    if rope_cols:
        def store(n_roped_heads):
            for g in range(n_roped_heads):
                sl = slice(g * HEAD_DIM, (g + 1) * HEAD_DIM)
                rot = pltpu.roll(acc[:, sl], HEAD_DIM // 2, 1)
                o_ref[:, sl] = (acc[:, sl] * cos_ref[...] + rot * sin_ref[...]).astype(o_ref.dtype)
            if n_roped_heads * HEAD_DIM < bn:
                o_ref[:, n_roped_heads * HEAD_DIM:] = acc[:, n_roped_heads * HEAD_DIM:].astype(o_ref.dtype)

        full_blocks, rem = divmod(rope_cols, bn)
        pl.when(j < full_blocks)(lambda: store(bn // HEAD_DIM))
        if rem:
            pl.when(j == full_blocks)(lambda: store(rem // HEAD_DIM))
        pl.when(j >= full_blocks + (1 if rem else 0))(lambda: store(0))
    else:
        o_ref[...] = acc.astype(o_ref.dtype)


def _rms_matmul(xs, gains, w, *, bm, bn, out_dtype, res=None, rope=None, rope_cols=0, emit_xn=False, name):
    T = xs[0].shape[0]
    seg_widths = tuple(int(x.shape[1]) for x in xs)
    K = sum(seg_widths)
    N = w.shape[1]
    norm = gains is not None
    assert w.shape[0] == K and T % bm == 0 and N % bn == 0 and rope_cols % HEAD_DIM == 0
    in_specs = [pl.BlockSpec((bm, kw), lambda i, j: (i, 0)) for kw in seg_widths]
    args = list(xs)
    if norm:
        in_specs += [pl.BlockSpec((1, kw), lambda i, j: (0, 0)) for kw in seg_widths]
        args += [g.reshape(1, -1).astype(F32) for g in gains]
    in_specs.append(pl.BlockSpec((K, bn), lambda i, j: (0, j), pipeline_mode=pl.Buffered(1) if bn == N else None))
    args.append(w)
    if res is not None:
        in_specs.append(pl.BlockSpec((bm, bn), lambda i, j: (i, j)))
        args.append(res)
    if rope_cols:
        cos, sin = rope
        pos_blocks = cos.shape[0] // bm
        in_specs += [pl.BlockSpec((bm, HEAD_DIM), lambda i, j: (i % pos_blocks, 0))] * 2
        args += [cos, sin]
    out_shape = [jax.ShapeDtypeStruct((T, N), out_dtype)]
    out_specs = [pl.BlockSpec((bm, bn), lambda i, j: (i, j))]
    if emit_xn:
        out_shape.append(jax.ShapeDtypeStruct((T, K), BF16))
        out_specs.append(pl.BlockSpec((bm, K), lambda i, j: (i, 0)))
    kern = functools.partial(_rms_matmul_kernel, seg_widths=seg_widths, norm=norm, has_res=res is not None,
                             rope_cols=rope_cols, emit_xn=emit_xn)
    outs = pl.pallas_call(
        kern,
        grid=(T // bm, N // bn),
        in_specs=in_specs,
        out_specs=out_specs,
        out_shape=out_shape,
        scratch_shapes=[pltpu.VMEM((bm, K), BF16)],
        compiler_params=_params(("arbitrary", "arbitrary")),
        name=name,
    )(*args)
    return outs if emit_xn else outs[0]


def _swa_kernel(q_ref, k_ref, v_ref, sink_ref, o_ref, *, seq, qblocks):
    band = 3 * SWA_BLOCK
    scale = HEAD_DIM ** -0.5
    sink = sink_ref[0]
    blocks = range(qblocks)
    rows = [slice(t * SWA_BLOCK, (t + 1) * SWA_BLOCK) for t in blocks]
    scores, values = [], []
    for t in blocks:
        n = pl.program_id(2) * qblocks + t
        start = pl.multiple_of(jnp.clip((n - 1) * SWA_BLOCK, 0, seq - band), SWA_BLOCK)
        k = k_ref[pl.ds(start, band), :]
        values.append(v_ref[pl.ds(start, band), :].astype(F32).T.astype(BF16))
        kpos = start + lax.broadcasted_iota(jnp.int32, (band, SWA_BLOCK), 0)
        qpos = n * SWA_BLOCK + lax.broadcasted_iota(jnp.int32, (band, SWA_BLOCK), 1)
        valid = jnp.abs(qpos - kpos) <= SWA_WINDOW
        q = jnp.concatenate([q_ref[rows[t], g * HEAD_DIM:(g + 1) * HEAD_DIM] for g in range(SWA_GROUP)], axis=0)
        s = lax.dot_general(k, q, (((1,), (1,)), ((), ())), preferred_element_type=F32) * scale
        scores.append(jnp.where(jnp.concatenate([valid] * SWA_GROUP, axis=1), s, NEG_INF))
    maxes = [jnp.maximum(jnp.max(s, axis=0, keepdims=True), sink) for s in scores]
    probs = [jnp.exp(s - m) for s, m in zip(scores, maxes)]
    denoms = [jnp.sum(p, axis=0, keepdims=True) + jnp.exp(sink - m) for p, m in zip(probs, maxes)]
    outs = [jnp.dot(vt, (p * (1.0 / d)).astype(BF16), preferred_element_type=F32)
            for p, d, vt in zip(probs, denoms, values)]
    for t in blocks:
        for g in range(SWA_GROUP):
            o_ref[rows[t], g * HEAD_DIM:(g + 1) * HEAD_DIM] = outs[t][:, g * SWA_BLOCK:(g + 1) * SWA_BLOCK].T


def _swa_attention(qkv, sink, batch, seq, *, qblocks=4):
    T = batch * seq
    nb = seq // (SWA_BLOCK * qblocks)
    gw = SWA_GROUP * HEAD_DIM
    k_col0 = SWA_Q_W // HEAD_DIM
    v_col0 = (SWA_Q_W + SWA_KV_W) // HEAD_DIM
    sink_b = jnp.broadcast_to(sink.astype(F32).reshape(SWA_KV_HEADS, 1, SWA_GROUP, 1),
                              (SWA_KV_HEADS, 1, SWA_GROUP, SWA_BLOCK)).reshape(SWA_KV_HEADS, 1, gw)
    return pl.pallas_call(
        functools.partial(_swa_kernel, seq=seq, qblocks=qblocks),
        grid=(batch, SWA_KV_HEADS, nb),
        in_specs=[
            pl.BlockSpec((SWA_BLOCK * qblocks, gw), lambda b, h, n: (b * nb + n, h)),
            pl.BlockSpec((seq, HEAD_DIM), lambda b, h, n: (b, k_col0 + h)),
            pl.BlockSpec((seq, HEAD_DIM), lambda b, h, n: (b, v_col0 + h)),
            pl.BlockSpec((1, 1, gw), lambda b, h, n: (h, 0, 0)),
        ],
        out_specs=pl.BlockSpec((SWA_BLOCK * qblocks, gw), lambda b, h, n: (b * nb + n, h)),
        out_shape=jax.ShapeDtypeStruct((T, SWA_Q_W), F32),
        compiler_params=_params(("arbitrary", "arbitrary", "arbitrary")),
        name="swa_attention",
    )(qkv, qkv, qkv, sink_b)


def _nat_tables(seq):
    rows = seq // GRID_W
    kr = min(NAT_KR_MAX, rows)
    span = min(rows, NAT_QROWS + kr - 1)
    nb = rows // NAT_QROWS
    qbl = NAT_QROWS * GRID_W
    r0 = np.arange(nb) * NAT_QROWS
    kstart = np.clip(r0 - kr // 2, 0, rows - span)
    key_rows = kstart[:, None] + np.arange(span)[None, :]
    q_row = r0[:, None] + (np.arange(qbl) // GRID_W)[None, :]
    q_col = np.broadcast_to((np.arange(qbl) % GRID_W)[None, :], q_row.shape)
    k_row = np.repeat(key_rows, GRID_W, axis=1)
    k_col = np.broadcast_to(np.tile(np.arange(GRID_W), span)[None, :], k_row.shape)
    rs = np.clip(q_row - kr // 2, 0, rows - kr)[:, :, None]
    cs = np.clip(q_col - NAT_KC // 2, 0, GRID_W - NAT_KC)[:, :, None]
    kr3, kc3 = k_row[:, None, :], k_col[:, None, :]
    valid = (kr3 >= rs) & (kr3 < rs + kr) & (kc3 >= cs) & (kc3 < cs + NAT_KC)
    dr = np.clip(kr3 - q_row[:, :, None] + NAT_KR_MAX - 1, 0, 2 * NAT_KR_MAX - 2)
    dc = np.clip(kc3 - q_col[:, :, None] + NAT_KC - 1, 0, 2 * NAT_KC - 2)
    flat = np.where(valid, dr * (2 * NAT_KC - 1) + dc, -1).reshape(nb, -1)
    _, first, inverse = np.unique(flat, axis=0, return_index=True, return_inverse=True)
    valid, dr, dc = valid[first], dr[first], dc[first]
    dr_tab = dr[:, ::GRID_W, ::GRID_W]
    assert (dr == np.repeat(np.repeat(dr_tab, GRID_W, axis=1), GRID_W, axis=2)).all()
    col = np.arange(GRID_W)
    toeplitz = np.clip(col[None, :] - col[:, None] + NAT_KC - 1, 0, 2 * NAT_KC - 2)
    assert (dc == np.tile(toeplitz, (NAT_QROWS, span))[None]).all()
    far = np.tile(np.abs(col[None, :] - col[:, None]) >= NAT_KC, (NAT_QROWS, span))
    assert not (valid & far[None]).any()
    return dict(nb=nb, qbl=qbl, span=span, nkeys=span * GRID_W, start=(kstart * GRID_W).astype(np.int32),
                pid=np.asarray(inverse).reshape(-1).astype(np.int32), valid=valid, dr_tab=dr_tab)


def _nat_bias_kernel(rows_ref, valid_ref, o_ref, *, span):
    n_pairs = -(-span // 2)
    for qr in range(NAT_QROWS):
        rs = slice(qr * GRID_W, (qr + 1) * GRID_W)
        for pair in range(n_pairs):
            row = rows_ref[0, 0, qr * n_pairs + pair:qr * n_pairs + pair + 1, :]
            blk = pltpu.roll(jnp.broadcast_to(row, (GRID_W, LANES)), LANES - (NAT_KC - 1), 1,
                             stride=1, stride_axis=0)
            width = min(LANES, span * GRID_W - pair * LANES)
            cs = slice(pair * LANES, pair * LANES + width)
            o_ref[0, 0, rs, cs] = jnp.where(valid_ref[0, rs, cs] > 0, blk[:, :width], NEG_INF)


def _nat_bias_table(rel_bias, tab):
    span, qbl, nkeys = tab["span"], tab["qbl"], tab["nkeys"]
    n_pat = tab["valid"].shape[0]
    n_pairs = -(-span // 2)
    nrel = 2 * NAT_KC - 1
    assert 2 * GRID_W == LANES and nrel <= GRID_W
    rows = rel_bias.astype(F32)[:, tab["dr_tab"]]
    rows = jnp.pad(rows, ((0, 0), (0, 0), (0, 0), (0, 2 * n_pairs - span), (0, GRID_W - nrel)))
    rows = jnp.transpose(rows.reshape(NAT_HEADS, n_pat, NAT_QROWS * n_pairs, LANES), (1, 0, 2, 3))
    valid = jnp.asarray(tab["valid"].astype(np.float32))
    return pl.pallas_call(
        functools.partial(_nat_bias_kernel, span=span),
        grid=(n_pat, NAT_HEADS),
        in_specs=[
            pl.BlockSpec((1, 1, NAT_QROWS * n_pairs, LANES), lambda p, h: (p, h, 0, 0)),
            pl.BlockSpec((1, qbl, nkeys), lambda p, h: (p, 0, 0)),
        ],
        out_specs=pl.BlockSpec((1, 1, qbl, nkeys), lambda p, h: (p, h, 0, 0)),
        out_shape=jax.ShapeDtypeStruct((n_pat, NAT_HEADS, qbl, nkeys), F32),
        compiler_params=_params(("arbitrary", "arbitrary")),
        name="nat_bias_table",
    )(rows, valid)


def _nat_kernel(pid_ref, start_ref, q_ref, k_ref, v_ref, *rest, nkeys, heads, qbl):
    del pid_ref
    bias_refs, o_ref = rest[:-1], rest[-1]
    scale = HEAD_DIM ** -0.5
    sls = [slice(h * HEAD_DIM, (h + 1) * HEAD_DIM) for h in range(heads)]
    blocks = range(len(bias_refs))
    rows = [slice(t * qbl, (t + 1) * qbl) for t in blocks]
    scores, values = [], []
    for t in blocks:
        start = pl.multiple_of(start_ref[pl.program_id(2) * len(bias_refs) + t], GRID_W)
        q = jnp.stack([q_ref[rows[t], sl] for sl in sls], axis=0)
        k = jnp.stack([k_ref[pl.ds(start, nkeys), sl] for sl in sls], axis=0)
        values.append(jnp.stack([v_ref[pl.ds(start, nkeys), sl] for sl in sls], axis=0))
        s = lax.dot_general(q, k, (((2,), (2,)), ((0,), (0,))), preferred_element_type=F32) * scale
        scores.append(s + bias_refs[t][0])
    maxes = [jnp.max(s, axis=-1, keepdims=True) for s in scores]
    probs = [jnp.exp(s - m) for s, m in zip(scores, maxes)]
    probs = [(p * (1.0 / jnp.sum(p, axis=-1, keepdims=True))).astype(BF16) for p in probs]
    outs = [lax.dot_general(p, v, (((2,), (1,)), ((0,), (0,))), preferred_element_type=F32)
            for p, v in zip(probs, values)]
    for t in blocks:
        for h, sl in enumerate(sls):
            o_ref[rows[t], sl] = outs[t][h]


def _nat_attention(qkv, rel_bias, batch, seq, *, qblocks=4):
    T = batch * seq
    tab = _nat_tables(seq)
    qbl, nkeys = tab["qbl"], tab["nkeys"]
    nb = tab["nb"] // qblocks
    bias = _nat_bias_table(rel_bias, tab)
    hpg = 4
    gw = hpg * HEAD_DIM
    n_hg = NAT_HEADS // hpg
    q0 = (SWA_Q_W + 2 * SWA_KV_W) // gw
    k0 = q0 + n_hg
    v0 = k0 + n_hg

    def bias_spec(t):
        return pl.BlockSpec((1, hpg, qbl, nkeys), lambda b, g, n, pid, st: (pid[n * qblocks + t], g, 0, 0))

    grid_spec = pltpu.PrefetchScalarGridSpec(
        num_scalar_prefetch=2,
        grid=(batch, n_hg, nb),
        in_specs=[
            pl.BlockSpec((qbl * qblocks, gw), lambda b, g, n, pid, st: (b * nb + n, q0 + g)),
            pl.BlockSpec((seq, gw), lambda b, g, n, pid, st: (b, k0 + g)),
            pl.BlockSpec((seq, gw), lambda b, g, n, pid, st: (b, v0 + g)),
        ] + [bias_spec(t) for t in range(qblocks)],
        out_specs=pl.BlockSpec((qbl * qblocks, gw), lambda b, g, n, pid, st: (b * nb + n, g)),
    )
    return pl.pallas_call(
        functools.partial(_nat_kernel, nkeys=nkeys, heads=hpg, qbl=qbl),
        grid_spec=grid_spec,
        out_shape=jax.ShapeDtypeStruct((T, NAT_W), F32),
        compiler_params=_params(("arbitrary", "arbitrary", "arbitrary")),
        name="nat_attention",
    )(jnp.asarray(tab["pid"]), jnp.asarray(tab["start"]), qkv, qkv, qkv, *([bias] * qblocks))


def _mem_attn_kernel(q_ref, kv_ref, o_ref):
    scale = MEM_HEAD_DIM ** -0.5
    for h in range(MEM_HEADS):
        sl = slice(h * MEM_HEAD_DIM, (h + 1) * MEM_HEAD_DIM)
        q = q_ref[:, sl]
        k = kv_ref[:, sl]
        v = kv_ref[:, MEM_W + h * MEM_HEAD_DIM:MEM_W + (h + 1) * MEM_HEAD_DIM]
        s = lax.dot_general(q, k, (((1,), (1,)), ((), ())), preferred_element_type=F32) * scale
        m = jnp.max(s, axis=-1, keepdims=True)
        p = jnp.exp(s - m)
        p = (p / jnp.sum(p, axis=-1, keepdims=True)).astype(BF16)
        o_ref[:, sl] = jnp.dot(p, v, preferred_element_type=F32).astype(o_ref.dtype)


def _mem_attention(q, kv, batch, seq, mem_len, *, bq=512):
    T = batch * seq
    nq = seq // bq
    return pl.pallas_call(
        _mem_attn_kernel,
        grid=(batch, nq),
        in_specs=[
            pl.BlockSpec((bq, MEM_W), lambda b, i: (b * nq + i, 0)),
            pl.BlockSpec((mem_len, 2 * MEM_W), lambda b, i: (b, 0)),
        ],
        out_specs=pl.BlockSpec((bq, MEM_W), lambda b, i: (b * nq + i, 0)),
        out_shape=jax.ShapeDtypeStruct((T, MEM_W), BF16),
        compiler_params=_params(("arbitrary", "arbitrary")),
        name="mem_attention",
    )(q, kv)


def _extract_max(work, rows):
    m = jnp.max(work, axis=0, keepdims=True)
    first = jnp.min(jnp.where(work == m, rows, float(work.shape[0])), axis=0, keepdims=True)
    return m, rows == first


def _top16_rows(scores, dests, work_scr, rank_scr, base):
    shape = scores[0].shape
    rows = lax.broadcasted_iota(jnp.int32, shape, 0).astype(F32)
    chains = range(len(scores))

    def run(first_only):
        for n, s in enumerate(scores):
            work_scr[base + n] = s
            rank_scr[base + n] = jnp.full(shape, float(PEER_TOPK), F32)

        def body(a, carry):
            for n in chains:
                work = work_scr[base + n]
                if first_only:
                    m, hit = _extract_max(work, rows)
                else:
                    m = jnp.max(work, axis=0, keepdims=True)
                    hit = work == m
                ref, idx = dests[n]
                ref[idx, pl.ds(a, 1), :] = m
                work_scr[base + n] = jnp.where(hit, -jnp.inf, work)
                rank_scr[base + n] = jnp.where(hit, jnp.asarray(a, F32), rank_scr[base + n])
            return carry

        lax.fori_loop(0, PEER_TOPK, body, 0)

    run(first_only=False)
    excess = jnp.zeros((1, shape[1]), F32)
    for n in chains:
        taken = jnp.sum(jnp.where(rank_scr[base + n] < float(PEER_TOPK), 1.0, 0.0), axis=0, keepdims=True)
        excess = jnp.maximum(excess, jnp.abs(taken - float(PEER_TOPK)))
    pl.when(jnp.max(excess) > 0.0)(lambda: run(first_only=True))
    return [rank_scr[base + n] for n in chains]


def _route_kernel(q_ref, keys_ref, n1_ref, e1_ref, r2_ref, e2_ref, v1_scr, v2_scr, c_scr, sel_scr, work_scr, rank_scr,
                  *, tb, group):
    half = PEER_QDIM // 2
    dn = (((1,), (1,)), ((), ()))
    s1_all = lax.dot_general(keys_ref[0, 0], q_ref[:, :half], dn, preferred_element_type=F32)
    s2_all = lax.dot_general(keys_ref[0, 1], q_ref[:, half:], dn, preferred_element_type=F32)
    n_c = tb // LANES
    cols = [slice(c * LANES, (c + 1) * LANES) for c in range(n_c)]
    s1 = [s1_all[:, cl] for cl in cols]
    s2 = [s2_all[:, cl] for cl in cols]
    rank1, rank2 = [], []
    for c0 in range(0, n_c, group):
        cs = list(range(c0, c0 + group))
        ranks = _top16_rows([s1[c] for c in cs] + [s2[c] for c in cs],
                            [(v1_scr, c) for c in cs] + [(v2_scr, c) for c in cs], work_scr, rank_scr, 2 * c0)
        rank1 += ranks[:group]
        rank2 += ranks[group:]
    for c in range(n_c):
        for p, (a, b) in enumerate(_CELLS):
            c_scr[c, p:p + 1, :] = v1_scr[c, a:a + 1, :] + v2_scr[c, b:b + 1, :]
        if _CELL_ROWS > _N_CELLS:
            c_scr[c, _N_CELLS:, :] = jnp.full((_CELL_ROWS - _N_CELLS, LANES), -jnp.inf, F32)
    cand = [c_scr[c] for c in range(n_c)]
    prow = lax.broadcasted_iota(jnp.int32, cand[0].shape, 0).astype(F32)

    def body(t, carry):
        out = []
        for work, picked in carry:
            _, hit = _extract_max(work, prow)
            out.append((jnp.where(hit, -jnp.inf, work), jnp.where(hit, 1.0, picked)))
        return tuple(out)

    picked = [pk for _, pk in lax.fori_loop(0, PEER_TOPK, body,
                                            tuple((cd, jnp.zeros(cd.shape, F32)) for cd in cand))]
    for c in range(n_c):
        top1 = v1_scr[c, 0:1, :]
        top2 = v2_scr[c, 0:1, :]
        z = jnp.sum(jnp.where(picked[c] > 0, jnp.exp(cand[c] - (top1 + top2)), 0.0), axis=0, keepdims=True)
        sel_scr[c] = picked[c]
        n1 = jnp.zeros(s1[c].shape, F32)
        p0 = 0
        for a in range(PEER_TOPK):
            width = sum(1 for (aa, _) in _CELLS if aa == a)
            n_a = jnp.sum(sel_scr[c, p0:p0 + width, :], axis=0, keepdims=True)
            n1 = jnp.where(rank1[c] == float(a), n_a, n1)
            p0 += width
        n1_ref[0, :, cols[c]] = n1
        e1_ref[0, :, cols[c]] = jnp.exp(s1[c] - top1) / z
        r2_ref[0, :, cols[c]] = rank2[c].astype(r2_ref.dtype)
        e2_ref[0, :, cols[c]] = jnp.exp(s2[c] - top2).astype(e2_ref.dtype)


def _peer_route(q, sub_keys, *, tb=512, group=4):
    T = q.shape[0]
    shape = jax.ShapeDtypeStruct((PEER_HEADS, PEER_NKEYS, T), F32)
    shape_packed = jax.ShapeDtypeStruct((PEER_HEADS, PEER_NKEYS, T), BF16)
    out_spec = pl.BlockSpec((1, PEER_NKEYS, tb), lambda i, h: (h, 0, i))
    return pl.pallas_call(
        functools.partial(_route_kernel, tb=tb, group=group),
        grid=(T // tb, PEER_HEADS),
        in_specs=[
            pl.BlockSpec((tb, PEER_QDIM), lambda i, h: (i, h)),
            pl.BlockSpec((1, 2, PEER_NKEYS, PEER_QDIM // 2), lambda i, h: (h, 0, 0, 0)),
        ],
        out_specs=[out_spec] * 4,
        out_shape=[shape, shape, shape_packed, shape_packed],
        scratch_shapes=[pltpu.VMEM((tb // LANES, PEER_TOPK, LANES), F32),
                        pltpu.VMEM((tb // LANES, PEER_TOPK, LANES), F32),
                        pltpu.VMEM((tb // LANES, _CELL_ROWS, LANES), F32),
                        pltpu.VMEM((tb // LANES, _CELL_ROWS, LANES), F32),
                        pltpu.VMEM((2 * (tb // LANES), PEER_NKEYS, LANES), F32),
                        pltpu.VMEM((2 * (tb // LANES), PEER_NKEYS, LANES), F32)],
        compiler_params=_params(("arbitrary", "arbitrary")),
        name="peer_route",
    )(q, sub_keys)


def _peer_kernel(xnt_ref, u_ref, vt_ref, n1_ref, e1_ref, r2_ref, e2_ref, o_ref, a_scr, w_scr, *, eb, tc):
    k = pl.program_id(1)

    @pl.when(k == 0)
    def _():
        o_ref[...] = jnp.zeros(o_ref.shape, F32)
        a_scr[...] = jnp.dot(u_ref[...], xnt_ref[...], preferred_element_type=F32)

    @pl.when(k > 0)
    def _():
        n_i = eb // PEER_NKEYS
        pack = 16
        groups = PEER_NKEYS // pack
        for c in range(a_scr.shape[1] // tc):
            cols = slice(c * tc, (c + 1) * tc)
            for a in range(n_i):
                rows = slice(a * PEER_NKEYS, (a + 1) * PEER_NKEYS)
                i_row = k * n_i + (a - n_i)
                a_t = a_scr[rows, cols]
                act = 0.5 * a_t * (1.0 + lax.erf(a_t * np.float32(np.sqrt(0.5))))
                gate = jnp.zeros((PEER_NKEYS, tc), BF16)
                for h in range(PEER_HEADS):
                    cnt = jnp.broadcast_to(n1_ref[h, pl.ds(i_row, 1), :][:, cols], (pack, tc)).astype(BF16)
                    g1 = jnp.broadcast_to(e1_ref[h, pl.ds(i_row, 1), :][:, cols], (pack, tc)).astype(BF16)
                    cnt = jnp.concatenate([cnt] * groups, axis=0)
                    g1 = jnp.concatenate([g1] * groups, axis=0)
                    gate = gate + jnp.where(r2_ref[h, :, cols] < cnt, e2_ref[h, :, cols] * g1,
                                            jnp.zeros((), BF16))
                w_scr[rows, cols] = (act * gate.astype(F32)).astype(BF16)
        a_scr[...] = jnp.dot(u_ref[...], xnt_ref[...], preferred_element_type=F32)
        o_ref[...] += jnp.dot(vt_ref[0], w_scr[...], preferred_element_type=F32)


def _peer_mix(xnt, u, vt, n1, e1, r2, e2, *, tb=512, tc=512):
    D, T = xnt.shape
    E = u.shape[0]
    n_e, _, eb = vt.shape
    once = pl.Buffered(1)
    route_spec = pl.BlockSpec((PEER_HEADS, PEER_NKEYS, tb), lambda i, k: (0, 0, i), pipeline_mode=once)
    return pl.pallas_call(
        functools.partial(_peer_kernel, eb=eb, tc=tc),
        grid=(T // tb, n_e + 1),
        in_specs=[
            pl.BlockSpec((D, tb), lambda i, k: (0, i), pipeline_mode=once),
            pl.BlockSpec((eb, D), lambda i, k: (jnp.minimum(k, n_e - 1), 0)),
            pl.BlockSpec((1, D, eb), lambda i, k: (jnp.maximum(k - 1, 0), 0, 0)),
            route_spec, route_spec, route_spec, route_spec,
        ],
        out_specs=pl.BlockSpec((D, tb), lambda i, k: (0, i)),
        out_shape=jax.ShapeDtypeStruct((D, T), F32),
        scratch_shapes=[pltpu.VMEM((eb, tb), F32), pltpu.VMEM((eb, tb), BF16)],
        compiler_params=_params(("arbitrary", "arbitrary")),
        name="peer_mix",
    )(xnt, u, vt, n1, e1, r2, e2)


def _add_norm_kernel(a_ref, bt_ref, g_ref, o_ref, *, norm):
    x = a_ref[...] + bt_ref[...].T
    if norm:
        ms = jnp.mean(x * x, axis=-1, keepdims=True)
        x = (x * lax.rsqrt(ms + RMS_EPS)) * g_ref[...]
    o_ref[...] = x


def _add_norm(a, bt, gain, *, norm, bm=256):
    T, D = a.shape
    spec = pl.BlockSpec((bm, D), lambda i: (i, 0))
    return pl.pallas_call(
        functools.partial(_add_norm_kernel, norm=norm),
        grid=(T // bm,),
        in_specs=[spec, pl.BlockSpec((D, bm), lambda i: (0, i)), pl.BlockSpec((1, D), lambda i: (0, 0))],
        out_specs=spec,
        out_shape=jax.ShapeDtypeStruct((T, D), F32),
        compiler_params=_params(("arbitrary",)),
        name="add_norm",
    )(a, bt, gain.reshape(1, D).astype(F32))


def _rope_tables(seq):
    half = HEAD_DIM // 2
    inv = ROPE_THETA ** (-jnp.arange(half, dtype=F32) / half)
    ang = jnp.arange(seq, dtype=jnp.int32).astype(F32)[:, None] * inv[None, :]
    cos, sin = jnp.cos(ang), jnp.sin(ang)
    return jnp.concatenate([cos, cos], axis=-1), jnp.concatenate([-sin, sin], axis=-1)


def kernel(x, mem, norm_mix, w_in, swa_sink, nat_rel_bias, out_norm_swa, out_norm_nat, w_out, norm_mem_q,
           norm_mem_kv, w_mem_q, w_mem_k, w_mem_v, w_mem_o, norm_ffn, w_peer_q, peer_sub_keys, peer_u, peer_v,
           norm_final):
    B, S, D = x.shape
    M = mem.shape[1]
    T = B * S
    depth = w_in.shape[0]
    assert depth >= 1
    rope = _rope_tables(S)
    h = x.reshape(T, D)
    mem2 = mem.reshape(B * M, D)
    rope_cols = SWA_Q_W + SWA_KV_W
    for l in range(depth):
        qkv = _rms_matmul([h], [norm_mix[l]], w_in[l].astype(BF16), bm=512, bn=1024, out_dtype=BF16,
                          rope=rope, rope_cols=rope_cols, name="in_proj")
        oa = _swa_attention(qkv, swa_sink[l], B, S)
        ob = _nat_attention(qkv, nat_rel_bias[l], B, S)
        h = _rms_matmul([oa, ob], [out_norm_swa[l], out_norm_nat[l]], w_out[l].astype(BF16), bm=512, bn=1024,
                        out_dtype=F32, res=h, name="out_proj")
        w_kv = jnp.concatenate([w_mem_k[l], w_mem_v[l]], axis=1).astype(BF16)
        kv = _rms_matmul([mem2], [norm_mem_kv[l]], w_kv, bm=512, bn=512, out_dtype=BF16, name="mem_kv_proj")
        qm = _rms_matmul([h], [norm_mem_q[l]], w_mem_q[l].astype(BF16), bm=512, bn=MEM_W, out_dtype=BF16,
                         name="mem_q_proj")
        om = _mem_attention(qm, kv, B, S, M)
        h = _rms_matmul([om], None, w_mem_o[l].astype(BF16), bm=512, bn=D, out_dtype=F32, res=h,
                        name="mem_o_proj")
        qp, xn = _rms_matmul([h], [norm_ffn[l]], w_peer_q[l].astype(BF16), bm=512, bn=PEER_HEADS * PEER_QDIM,
                             out_dtype=BF16, emit_xn=True, name="peer_q_proj")
        n1, e1, r2, e2 = _peer_route(qp, peer_sub_keys[l].astype(BF16))
        eb = 512
        vt = jnp.transpose(peer_v[l].reshape(-1, eb, D), (0, 2, 1)).astype(BF16)
        peer = _peer_mix(xn.T, peer_u[l].astype(BF16), vt, n1, e1, r2, e2)
        last = l == depth - 1
        h = _add_norm(h, peer, norm_final if last else jnp.ones((D,), F32), norm=last)
    return h.reshape(B, S, D)
```

```python
import functools

import numpy as np
import jax
import jax.numpy as jnp
from jax import lax
from jax.experimental import pallas as pl
from jax.experimental.pallas import tpu as pltpu

F32 = jnp.float32
BF16 = jnp.bfloat16

D_MODEL = 4096
HEAD_DIM = 128
SWA_Q_HEADS = 16
SWA_KV_HEADS = 4
SWA_GROUP = SWA_Q_HEADS // SWA_KV_HEADS
SWA_WINDOW = 128
SWA_BLOCK = 128
ROPE_THETA = 10000.0
NAT_HEADS = 16
NAT_KR_MAX = 8
NAT_KC = 16
GRID_W = 64
NAT_QROWS = 2
SWA_Q_W = SWA_Q_HEADS * HEAD_DIM
SWA_KV_W = SWA_KV_HEADS * HEAD_DIM
NAT_W = NAT_HEADS * HEAD_DIM
MEM_HEADS = 4
MEM_HEAD_DIM = 256
MEM_W = MEM_HEADS * MEM_HEAD_DIM
PEER_HEADS = 8
PEER_NKEYS = 128
PEER_QDIM = 256
PEER_TOPK = 16
RMS_EPS = 1e-6
NEG_INF = -1e30

LANES = 128
VMEM_LIMIT_BYTES = 56 * 1024 * 1024

_CELLS = [(a, b) for a in range(PEER_TOPK) for b in range(PEER_TOPK) if (a + 1) * (b + 1) <= PEER_TOPK]
_N_CELLS = len(_CELLS)
_CELL_ROWS = -(-_N_CELLS // 8) * 8


def _params(sem, vmem_limit_bytes=None):
    return pltpu.CompilerParams(dimension_semantics=sem, vmem_limit_bytes=vmem_limit_bytes or VMEM_LIMIT_BYTES)


def _rms_matmul_kernel(*refs, seg_widths, norm, has_res, rope_cols, emit_xn):
    n_seg = len(seg_widths)
    pos = 0
    x_refs = refs[pos:pos + n_seg]; pos += n_seg
    g_refs = ()
    if norm:
        g_refs = refs[pos:pos + n_seg]; pos += n_seg
    w_ref = refs[pos]; pos += 1
    res_ref = None
    if has_res:
        res_ref = refs[pos]; pos += 1
    cos_ref = sin_ref = None
    if rope_cols:
        cos_ref, sin_ref = refs[pos], refs[pos + 1]; pos += 2
    o_ref = refs[pos]; pos += 1
    xn_out_ref = None
    if emit_xn:
        xn_out_ref = refs[pos]; pos += 1
    xn_ref = refs[pos]

    j = pl.program_id(1)

    @pl.when(j == 0)
    def _():
        off = 0
        for s in range(n_seg):
            x = x_refs[s][...].astype(F32)
            if norm:
                ms = jnp.mean(x * x, axis=-1, keepdims=True)
                x = (x * lax.rsqrt(ms + RMS_EPS)) * g_refs[s][...]
            xn_ref[:, off:off + seg_widths[s]] = x.astype(BF16)
            off += seg_widths[s]
        if emit_xn:
            xn_out_ref[...] = xn_ref[...]

    acc = jnp.dot(xn_ref[...], w_ref[...], preferred_element_type=F32)
    if has_res:
        acc = acc + res_ref[...]

    bn = acc.shape[1]
    if rope_cols:
        def store(n_roped_heads):
            for g in range(n_roped_heads):
                sl = slice(g * HEAD_DIM, (g + 1) * HEAD_DIM)
                rot = pltpu.roll(acc[:, sl], HEAD_DIM // 2, 1)
                o_ref[:, sl] = (acc[:, sl] * cos_ref[...] + rot * sin_ref[...]).astype(o_ref.dtype)
            if n_roped_heads * HEAD_DIM < bn:
                o_ref[:, n_roped_heads * HEAD_DIM:] = acc[:, n_roped_heads * HEAD_DIM:].astype(o_ref.dtype)

        full_blocks, rem = divmod(rope_cols, bn)
        pl.when(j < full_blocks)(lambda: store(bn // HEAD_DIM))
        if rem:
            pl.when(j == full_blocks)(lambda: store(rem // HEAD_DIM))
        pl.when(j >= full_blocks + (1 if rem else 0))(lambda: store(0))
    else:
        o_ref[...] = acc.astype(o_ref.dtype)


def _rms_matmul(xs, gains, w, *, bm, bn, out_dtype, res=None, rope=None, rope_cols=0, emit_xn=False, name):
    T = xs[0].shape[0]
    seg_widths = tuple(int(x.shape[1]) for x in xs)
    K = sum(seg_widths)
    N = w.shape[1]
    norm = gains is not None
    assert w.shape[0] == K and T % bm == 0 and N % bn == 0 and rope_cols % HEAD_DIM == 0
    in_specs = [pl.BlockSpec((bm, kw), lambda i, j: (i, 0)) for kw in seg_widths]
    args = list(xs)
    if norm:
        in_specs += [pl.BlockSpec((1, kw), lambda i, j: (0, 0)) for kw in seg_widths]
        args += [g.reshape(1, -1).astype(F32) for g in gains]
    in_specs.append(pl.BlockSpec((K, bn), lambda i, j: (0, j), pipeline_mode=pl.Buffered(1) if bn == N else None))
    args.append(w)
    if res is not None:
        in_specs.append(pl.BlockSpec((bm, bn), lambda i, j: (i, j)))
        args.append(res)
    if rope_cols:
        cos, sin = rope
        pos_blocks = cos.shape[0] // bm
        in_specs += [pl.BlockSpec((bm, HEAD_DIM), lambda i, j: (i % pos_blocks, 0))] * 2
        args += [cos, sin]
    out_shape = [jax.ShapeDtypeStruct((T, N), out_dtype)]
    out_specs = [pl.BlockSpec((bm, bn), lambda i, j: (i, j))]
    if emit_xn:
        out_shape.append(jax.ShapeDtypeStruct((T, K), BF16))
        out_specs.append(pl.BlockSpec((bm, K), lambda i, j: (i, 0)))
    kern = functools.partial(_rms_matmul_kernel, seg_widths=seg_widths, norm=norm, has_res=res is not None,
                             rope_cols=rope_cols, emit_xn=emit_xn)
    outs = pl.pallas_call(
        kern,
        grid=(T // bm, N // bn),
        in_specs=in_specs,
        out_specs=out_specs,
        out_shape=out_shape,
        scratch_shapes=[pltpu.VMEM((bm, K), BF16)],
        compiler_params=_params(("arbitrary", "arbitrary")),
        name=name,
    )(*args)
    return outs if emit_xn else outs[0]


def _swa_kernel(q_ref, k_ref, v_ref, sink_ref, o_ref, *, seq, qblocks):
    band = 3 * SWA_BLOCK
    scale = HEAD_DIM ** -0.5
    sink = sink_ref[0]
    blocks = range(qblocks)
    rows = [slice(t * SWA_BLOCK, (t + 1) * SWA_BLOCK) for t in blocks]
    scores, values = [], []
    for t in blocks:
        n = pl.program_id(2) * qblocks + t
        start = pl.multiple_of(jnp.clip((n - 1) * SWA_BLOCK, 0, seq - band), SWA_BLOCK)
        k = k_ref[pl.ds(start, band), :]
        values.append(v_ref[pl.ds(start, band), :].astype(F32).T.astype(BF16))
        kpos = start + lax.broadcasted_iota(jnp.int32, (band, SWA_BLOCK), 0)
        qpos = n * SWA_BLOCK + lax.broadcasted_iota(jnp.int32, (band, SWA_BLOCK), 1)
        valid = jnp.abs(qpos - kpos) <= SWA_WINDOW
        q = jnp.concatenate([q_ref[rows[t], g * HEAD_DIM:(g + 1) * HEAD_DIM] for g in range(SWA_GROUP)], axis=0)
        s = lax.dot_general(k, q, (((1,), (1,)), ((), ())), preferred_element_type=F32) * scale
        scores.append(jnp.where(jnp.concatenate([valid] * SWA_GROUP, axis=1), s, NEG_INF))
    maxes = [jnp.maximum(jnp.max(s, axis=0, keepdims=True), sink) for s in scores]
    probs = [jnp.exp(s - m) for s, m in zip(scores, maxes)]
    denoms = [jnp.sum(p, axis=0, keepdims=True) + jnp.exp(sink - m) for p, m in zip(probs, maxes)]
    outs = [jnp.dot(vt, (p * (1.0 / d)).astype(BF16), preferred_element_type=F32)
            for p, d, vt in zip(probs, denoms, values)]
    for t in blocks:
        for g in range(SWA_GROUP):
            o_ref[rows[t], g * HEAD_DIM:(g + 1) * HEAD_DIM] = outs[t][:, g * SWA_BLOCK:(g + 1) * SWA_BLOCK].T


def _swa_attention(qkv, sink, batch, seq, *, qblocks=4):
    T = batch * seq
    nb = seq // (SWA_BLOCK * qblocks)
    gw = SWA_GROUP * HEAD_DIM
    k_col0 = SWA_Q_W // HEAD_DIM
    v_col0 = (SWA_Q_W + SWA_KV_W) // HEAD_DIM
    sink_b = jnp.broadcast_to(sink.astype(F32).reshape(SWA_KV_HEADS, 1, SWA_GROUP, 1),
                              (SWA_KV_HEADS, 1, SWA_GROUP, SWA_BLOCK)).reshape(SWA_KV_HEADS, 1, gw)
    return pl.pallas_call(
        functools.partial(_swa_kernel, seq=seq, qblocks=qblocks),
        grid=(batch, SWA_KV_HEADS, nb),
        in_specs=[
            pl.BlockSpec((SWA_BLOCK * qblocks, gw), lambda b, h, n: (b * nb + n, h)),
            pl.BlockSpec((seq, HEAD_DIM), lambda b, h, n: (b, k_col0 + h)),
            pl.BlockSpec((seq, HEAD_DIM), lambda b, h, n: (b, v_col0 + h)),
            pl.BlockSpec((1, 1, gw), lambda b, h, n: (h, 0, 0)),
        ],
        out_specs=pl.BlockSpec((SWA_BLOCK * qblocks, gw), lambda b, h, n: (b * nb + n, h)),
        out_shape=jax.ShapeDtypeStruct((T, SWA_Q_W), F32),
        compiler_params=_params(("arbitrary", "arbitrary", "arbitrary")),
        name="swa_attention",
    )(qkv, qkv, qkv, sink_b)


def _nat_tables(seq):
    rows = seq // GRID_W
    kr = min(NAT_KR_MAX, rows)
    span = min(rows, NAT_QROWS + kr - 1)
    nb = rows // NAT_QROWS
    qbl = NAT_QROWS * GRID_W
    r0 = np.arange(nb) * NAT_QROWS
    kstart = np.clip(r0 - kr // 2, 0, rows - span)
    key_rows = kstart[:, None] + np.arange(span)[None, :]
    q_row = r0[:, None] + (np.arange(qbl) // GRID_W)[None, :]
    q_col = np.broadcast_to((np.arange(qbl) % GRID_W)[None, :], q_row.shape)
    k_row = np.repeat(key_rows, GRID_W, axis=1)
    k_col = np.broadcast_to(np.tile(np.arange(GRID_W), span)[None, :], k_row.shape)
    rs = np.clip(q_row - kr // 2, 0, rows - kr)[:, :, None]
    cs = np.clip(q_col - NAT_KC // 2, 0, GRID_W - NAT_KC)[:, :, None]
    kr3, kc3 = k_row[:, None, :], k_col[:, None, :]
    valid = (kr3 >= rs) & (kr3 < rs + kr) & (kc3 >= cs) & (kc3 < cs + NAT_KC)
    dr = np.clip(kr3 - q_row[:, :, None] + NAT_KR_MAX - 1, 0, 2 * NAT_KR_MAX - 2)
    dc = np.clip(kc3 - q_col[:, :, None] + NAT_KC - 1, 0, 2 * NAT_KC - 2)
    flat = np.where(valid, dr * (2 * NAT_KC - 1) + dc, -1).reshape(nb, -1)
    _, first, inverse = np.unique(flat, axis=0, return_index=True, return_inverse=True)
    valid, dr, dc = valid[first], dr[first], dc[first]
    dr_tab = dr[:, ::GRID_W, ::GRID_W]
    assert (dr == np.repeat(np.repeat(dr_tab, GRID_W, axis=1), GRID_W, axis=2)).all()
    col = np.arange(GRID_W)
    toeplitz = np.clip(col[None, :] - col[:, None] + NAT_KC - 1, 0, 2 * NAT_KC - 2)
    assert (dc == np.tile(toeplitz, (NAT_QROWS, span))[None]).all()
    far = np.tile(np.abs(col[None, :] - col[:, None]) >= NAT_KC, (NAT_QROWS, span))
    assert not (valid & far[None]).any()
    return dict(nb=nb, qbl=qbl, span=span, nkeys=span * GRID_W, start=(kstart * GRID_W).astype(np.int32),
                pid=np.asarray(inverse).reshape(-1).astype(np.int32), valid=valid, dr_tab=dr_tab)


def _nat_bias_kernel(rows_ref, valid_ref, o_ref, *, span):
    n_pairs = -(-span // 2)
    for h in range(o_ref.shape[1]):
        for qr in range(NAT_QROWS):
            rs = slice(qr * GRID_W, (qr + 1) * GRID_W)
            for pair in range(n_pairs):
                row = rows_ref[0, h, qr * n_pairs + pair:qr * n_pairs + pair + 1, :]
                blk = pltpu.roll(jnp.broadcast_to(row, (GRID_W, LANES)), LANES - (NAT_KC - 1), 1,
                                 stride=1, stride_axis=0)
                width = min(LANES, span * GRID_W - pair * LANES)
                cs = slice(pair * LANES, pair * LANES + width)
                o_ref[0, h, rs, cs] = jnp.where(valid_ref[0, rs, cs] > 0, blk[:, :width], NEG_INF)


def _nat_bias_table(rel_bias, tab):
    span, qbl, nkeys = tab["span"], tab["qbl"], tab["nkeys"]
    n_pat = tab["valid"].shape[0]
    n_pairs = -(-span // 2)
    nrel = 2 * NAT_KC - 1
    n_rows = -(-NAT_QROWS * n_pairs // 8) * 8
    assert 2 * GRID_W == LANES and nrel <= GRID_W
    rows = rel_bias.astype(F32)[:, tab["dr_tab"]]
    rows = jnp.pad(rows, ((0, 0), (0, 0), (0, 0), (0, 2 * n_pairs - span), (0, GRID_W - nrel)))
    rows = jnp.transpose(rows.reshape(NAT_HEADS, n_pat, NAT_QROWS * n_pairs, LANES), (1, 0, 2, 3))
    rows = jnp.pad(rows, ((0, 0), (0, 0), (0, n_rows - NAT_QROWS * n_pairs), (0, 0)))
    valid = jnp.asarray(tab["valid"].astype(np.float32))
    return pl.pallas_call(
        functools.partial(_nat_bias_kernel, span=span),
        grid=(n_pat,),
        in_specs=[
            pl.BlockSpec((1, NAT_HEADS, n_rows, LANES), lambda p: (p, 0, 0, 0)),
            pl.BlockSpec((1, qbl, nkeys), lambda p: (p, 0, 0)),
        ],
        out_specs=pl.BlockSpec((1, NAT_HEADS, qbl, nkeys), lambda p: (p, 0, 0, 0)),
        out_shape=jax.ShapeDtypeStruct((n_pat, NAT_HEADS, qbl, nkeys), F32),
        compiler_params=_params(("arbitrary",)),
        name="nat_bias_table",
    )(rows, valid)


def _nat_kernel(pid_ref, start_ref, q_ref, k_ref, v_ref, *rest, nkeys, heads, qbl):
    del pid_ref
    bias_refs, o_ref = rest[:-1], rest[-1]
    scale = HEAD_DIM ** -0.5
    sls = [slice(h * HEAD_DIM, (h + 1) * HEAD_DIM) for h in range(heads)]
    blocks = range(len(bias_refs))
    rows = [slice(t * qbl, (t + 1) * qbl) for t in blocks]
    scores, values = [], []
    for t in blocks:
        start = pl.multiple_of(start_ref[pl.program_id(2) * len(bias_refs) + t], GRID_W)
        q = jnp.stack([q_ref[rows[t], sl] for sl in sls], axis=0)
        k = jnp.stack([k_ref[pl.ds(start, nkeys), sl] for sl in sls], axis=0)
        values.append(jnp.stack([v_ref[pl.ds(start, nkeys), sl] for sl in sls], axis=0))
        s = lax.dot_general(q, k, (((2,), (2,)), ((0,), (0,))), preferred_element_type=F32) * scale
        scores.append(s + bias_refs[t][0])
    maxes = [jnp.max(s, axis=-1, keepdims=True) for s in scores]
    probs = [jnp.exp(s - m) for s, m in zip(scores, maxes)]
    probs = [(p * (1.0 / jnp.sum(p, axis=-1, keepdims=True))).astype(BF16) for p in probs]
    outs = [lax.dot_general(p, v, (((2,), (1,)), ((0,), (0,))), preferred_element_type=F32)
            for p, v in zip(probs, values)]
    for t in blocks:
        for h, sl in enumerate(sls):
            o_ref[rows[t], sl] = outs[t][h]


def _nat_attention(qkv, rel_bias, batch, seq, *, qblocks=4):
    T = batch * seq
    tab = _nat_tables(seq)
    qbl, nkeys = tab["qbl"], tab["nkeys"]
    nb = tab["nb"] // qblocks
    bias = _nat_bias_table(rel_bias, tab)
    hpg = 4
    gw = hpg * HEAD_DIM
    n_hg = NAT_HEADS // hpg
    q0 = (SWA_Q_W + 2 * SWA_KV_W) // gw
    k0 = q0 + n_hg
    v0 = k0 + n_hg

    def bias_spec(t):
        return pl.BlockSpec((1, hpg, qbl, nkeys), lambda b, g, n, pid, st: (pid[n * qblocks + t], g, 0, 0))

    grid_spec = pltpu.PrefetchScalarGridSpec(
        num_scalar_prefetch=2,
        grid=(batch, n_hg, nb),
        in_specs=[
            pl.BlockSpec((qbl * qblocks, gw), lambda b, g, n, pid, st: (b * nb + n, q0 + g)),
            pl.BlockSpec((seq, gw), lambda b, g, n, pid, st: (b, k0 + g)),
            pl.BlockSpec((seq, gw), lambda b, g, n, pid, st: (b, v0 + g)),
        ] + [bias_spec(t) for t in range(qblocks)],
        out_specs=pl.BlockSpec((qbl * qblocks, gw), lambda b, g, n, pid, st: (b * nb + n, g)),
    )
    return pl.pallas_call(
        functools.partial(_nat_kernel, nkeys=nkeys, heads=hpg, qbl=qbl),
        grid_spec=grid_spec,
        out_shape=jax.ShapeDtypeStruct((T, NAT_W), F32),
        compiler_params=_params(("arbitrary", "arbitrary", "arbitrary")),
        name="nat_attention",
    )(jnp.asarray(tab["pid"]), jnp.asarray(tab["start"]), qkv, qkv, qkv, *([bias] * qblocks))


def _mem_attn_kernel(q_ref, kv_ref, o_ref):
    scale = MEM_HEAD_DIM ** -0.5
    for h in range(MEM_HEADS):
        sl = slice(h * MEM_HEAD_DIM, (h + 1) * MEM_HEAD_DIM)
        q = q_ref[:, sl]
        k = kv_ref[:, sl]
        v = kv_ref[:, MEM_W + h * MEM_HEAD_DIM:MEM_W + (h + 1) * MEM_HEAD_DIM]
        s = lax.dot_general(q, k, (((1,), (1,)), ((), ())), preferred_element_type=F32) * scale
        m = jnp.max(s, axis=-1, keepdims=True)
        p = jnp.exp(s - m)
        p = (p / jnp.sum(p, axis=-1, keepdims=True)).astype(BF16)
        o_ref[:, sl] = jnp.dot(p, v, preferred_element_type=F32).astype(o_ref.dtype)


def _mem_attention(q, kv, batch, seq, mem_len, *, bq=512):
    T = batch * seq
    nq = seq // bq
    return pl.pallas_call(
        _mem_attn_kernel,
        grid=(batch, nq),
        in_specs=[
            pl.BlockSpec((bq, MEM_W), lambda b, i: (b * nq + i, 0)),
            pl.BlockSpec((mem_len, 2 * MEM_W), lambda b, i: (b, 0)),
        ],
        out_specs=pl.BlockSpec((bq, MEM_W), lambda b, i: (b * nq + i, 0)),
        out_shape=jax.ShapeDtypeStruct((T, MEM_W), BF16),
        compiler_params=_params(("arbitrary", "arbitrary")),
        name="mem_attention",
    )(q, kv)


def _extract_max(work, rows):
    m = jnp.max(work, axis=0, keepdims=True)
    first = jnp.min(jnp.where(work == m, rows, float(work.shape[0])), axis=0, keepdims=True)
    return m, rows == first


def _top16_rows(scores, dests, work_scr, rank_scr, base):
    shape = scores[0].shape
    rows = lax.broadcasted_iota(jnp.int32, shape, 0).astype(F32)
    chains = range(len(scores))

    def run(first_only):
        for n, s in enumerate(scores):
            work_scr[base + n] = s
            rank_scr[base + n] = jnp.full(shape, float(PEER_TOPK), F32)

        def body(a, carry):
            for n in chains:
                work = work_scr[base + n]
                if first_only:
                    m, hit = _extract_max(work, rows)
                else:
                    m = jnp.max(work, axis=0, keepdims=True)
                    hit = work == m
                ref, idx = dests[n]
                ref[idx, pl.ds(a, 1), :] = m
                work_scr[base + n] = jnp.where(hit, -jnp.inf, work)
                rank_scr[base + n] = jnp.where(hit, jnp.asarray(a, F32), rank_scr[base + n])
            return carry

        lax.fori_loop(0, PEER_TOPK, body, 0)

    run(first_only=False)
    excess = jnp.zeros((1, shape[1]), F32)
    for n in chains:
        taken = jnp.sum(jnp.where(rank_scr[base + n] < float(PEER_TOPK), 1.0, 0.0), axis=0, keepdims=True)
        excess = jnp.maximum(excess, jnp.abs(taken - float(PEER_TOPK)))
    pl.when(jnp.max(excess) > 0.0)(lambda: run(first_only=True))
    return [rank_scr[base + n] for n in chains]


def _route_kernel(q_ref, keys_ref, n1_ref, e1_ref, r2_ref, e2_ref, v1_scr, v2_scr, c_scr, sel_scr, work_scr, rank_scr,
                  *, tb, group):
    half = PEER_QDIM // 2
    dn = (((1,), (1,)), ((), ()))
    s1_all = lax.dot_general(keys_ref[0, 0], q_ref[:, :half], dn, preferred_element_type=F32)
    s2_all = lax.dot_general(keys_ref[0, 1], q_ref[:, half:], dn, preferred_element_type=F32)
    n_c = tb // LANES
    cols = [slice(c * LANES, (c + 1) * LANES) for c in range(n_c)]
    s1 = [s1_all[:, cl] for cl in cols]
    s2 = [s2_all[:, cl] for cl in cols]
    rank1, rank2 = [], []
    for c0 in range(0, n_c, group):
        cs = list(range(c0, c0 + group))
        ranks = _top16_rows([s1[c] for c in cs] + [s2[c] for c in cs],
                            [(v1_scr, c) for c in cs] + [(v2_scr, c) for c in cs], work_scr, rank_scr, 2 * c0)
        rank1 += ranks[:group]
        rank2 += ranks[group:]
    for c in range(n_c):
        for p, (a, b) in enumerate(_CELLS):
            c_scr[c, p:p + 1, :] = v1_scr[c, a:a + 1, :] + v2_scr[c, b:b + 1, :]
        if _CELL_ROWS > _N_CELLS:
            c_scr[c, _N_CELLS:, :] = jnp.full((_CELL_ROWS - _N_CELLS, LANES), -jnp.inf, F32)
    cand = [c_scr[c] for c in range(n_c)]
    prow = lax.broadcasted_iota(jnp.int32, cand[0].shape, 0).astype(F32)

    def body(t, carry):
        out = []
        for work, picked in carry:
            _, hit = _extract_max(work, prow)
            out.append((jnp.where(hit, -jnp.inf, work), jnp.where(hit, 1.0, picked)))
        return tuple(out)

    picked = [pk for _, pk in lax.fori_loop(0, PEER_TOPK, body,
                                            tuple((cd, jnp.zeros(cd.shape, F32)) for cd in cand))]
    for c in range(n_c):
        top1 = v1_scr[c, 0:1, :]
        top2 = v2_scr[c, 0:1, :]
        z = jnp.sum(jnp.where(picked[c] > 0, jnp.exp(cand[c] - (top1 + top2)), 0.0), axis=0, keepdims=True)
        sel_scr[c] = picked[c]
        n1 = jnp.zeros(s1[c].shape, F32)
        p0 = 0
        for a in range(PEER_TOPK):
            width = sum(1 for (aa, _) in _CELLS if aa == a)
            n_a = jnp.sum(sel_scr[c, p0:p0 + width, :], axis=0, keepdims=True)
            n1 = jnp.where(rank1[c] == float(a), n_a, n1)
            p0 += width
        n1_ref[0, :, cols[c]] = n1
        e1_ref[0, :, cols[c]] = jnp.exp(s1[c] - top1) / z
        r2_ref[0, :, cols[c]] = rank2[c].astype(r2_ref.dtype)
        e2_ref[0, :, cols[c]] = jnp.exp(s2[c] - top2).astype(e2_ref.dtype)


def _peer_route(q, sub_keys, *, tb=512, group=4):
    T = q.shape[0]
    shape = jax.ShapeDtypeStruct((PEER_HEADS, PEER_NKEYS, T), F32)
    shape_packed = jax.ShapeDtypeStruct((PEER_HEADS, PEER_NKEYS, T), BF16)
    out_spec = pl.BlockSpec((1, PEER_NKEYS, tb), lambda i, h: (h, 0, i))
    return pl.pallas_call(
        functools.partial(_route_kernel, tb=tb, group=group),
        grid=(T // tb, PEER_HEADS),
        in_specs=[
            pl.BlockSpec((tb, PEER_QDIM), lambda i, h: (i, h)),
            pl.BlockSpec((1, 2, PEER_NKEYS, PEER_QDIM // 2), lambda i, h: (h, 0, 0, 0)),
        ],
        out_specs=[out_spec] * 4,
        out_shape=[shape, shape, shape_packed, shape_packed],
        scratch_shapes=[pltpu.VMEM((tb // LANES, PEER_TOPK, LANES), F32),
                        pltpu.VMEM((tb // LANES, PEER_TOPK, LANES), F32),
                        pltpu.VMEM((tb // LANES, _CELL_ROWS, LANES), F32),
                        pltpu.VMEM((tb // LANES, _CELL_ROWS, LANES), F32),
                        pltpu.VMEM((2 * (tb // LANES), PEER_NKEYS, LANES), F32),
                        pltpu.VMEM((2 * (tb // LANES), PEER_NKEYS, LANES), F32)],
        compiler_params=_params(("arbitrary", "arbitrary")),
        name="peer_route",
    )(q, sub_keys)


def _peer_kernel(xnt_ref, u_ref, vt_ref, n1_ref, e1_ref, r2_ref, e2_ref, o_ref, a_scr, w_scr, *, eb, tc):
    k = pl.program_id(1)

    @pl.when(k == 0)
    def _():
        o_ref[...] = jnp.zeros(o_ref.shape, F32)
        a_scr[...] = jnp.dot(u_ref[...], xnt_ref[...], preferred_element_type=F32)

    @pl.when(k > 0)
    def _():
        n_i = eb // PEER_NKEYS
        pack = 16
        groups = PEER_NKEYS // pack
        for c in range(a_scr.shape[1] // tc):
            cols = slice(c * tc, (c + 1) * tc)
            for a in range(n_i):
                rows = slice(a * PEER_NKEYS, (a + 1) * PEER_NKEYS)
                i_row = k * n_i + (a - n_i)
                a_t = a_scr[rows, cols]
                act = 0.5 * a_t * (1.0 + lax.erf(a_t * np.float32(np.sqrt(0.5))))
                gate = jnp.zeros((PEER_NKEYS, tc), BF16)
                for h in range(PEER_HEADS):
                    cnt = jnp.broadcast_to(n1_ref[h, pl.ds(i_row, 1), :][:, cols], (pack, tc)).astype(BF16)
                    g1 = jnp.broadcast_to(e1_ref[h, pl.ds(i_row, 1), :][:, cols], (pack, tc)).astype(BF16)
                    cnt = jnp.concatenate([cnt] * groups, axis=0)
                    g1 = jnp.concatenate([g1] * groups, axis=0)
                    gate = gate + jnp.where(r2_ref[h, :, cols] < cnt, e2_ref[h, :, cols] * g1,
                                            jnp.zeros((), BF16))
                w_scr[rows, cols] = (act * gate.astype(F32)).astype(BF16)
        a_scr[...] = jnp.dot(u_ref[...], xnt_ref[...], preferred_element_type=F32)
        o_ref[...] += jnp.dot(vt_ref[0], w_scr[...], preferred_element_type=F32)


def _peer_mix(xnt, u, vt, n1, e1, r2, e2, *, tb=512, tc=512):
    D, T = xnt.shape
    E = u.shape[0]
    n_e, _, eb = vt.shape
    once = pl.Buffered(1)
    route_spec = pl.BlockSpec((PEER_HEADS, PEER_NKEYS, tb), lambda i, k: (0, 0, i), pipeline_mode=once)
    return pl.pallas_call(
        functools.partial(_peer_kernel, eb=eb, tc=tc),
        grid=(T // tb, n_e + 1),
        in_specs=[
            pl.BlockSpec((D, tb), lambda i, k: (0, i), pipeline_mode=once),
            pl.BlockSpec((eb, D), lambda i, k: (jnp.minimum(k, n_e - 1), 0)),
            pl.BlockSpec((1, D, eb), lambda i, k: (jnp.maximum(k - 1, 0), 0, 0)),
            route_spec, route_spec, route_spec, route_spec,
        ],
        out_specs=pl.BlockSpec((D, tb), lambda i, k: (0, i)),
        out_shape=jax.ShapeDtypeStruct((D, T), F32),
        scratch_shapes=[pltpu.VMEM((eb, tb), F32), pltpu.VMEM((eb, tb), BF16)],
        compiler_params=_params(("arbitrary", "arbitrary")),
        name="peer_mix",
    )(xnt, u, vt, n1, e1, r2, e2)


def _transpose_cast_kernel(x_ref, o_ref):
    o_ref[0] = x_ref[...].T.astype(o_ref.dtype)


def _transpose_cast_tiles(x, rows):
    N, D = x.shape
    return pl.pallas_call(
        _transpose_cast_kernel,
        grid=(N // rows,),
        in_specs=[pl.BlockSpec((rows, D), lambda e: (e, 0))],
        out_specs=pl.BlockSpec((1, D, rows), lambda e: (e, 0, 0)),
        out_shape=jax.ShapeDtypeStruct((N // rows, D, rows), BF16),
        compiler_params=_params(("arbitrary",)),
        name="transpose_cast",
    )(x)


def _add_norm_kernel(a_ref, bt_ref, g_ref, o_ref, *, norm):
    x = a_ref[...] + bt_ref[...].T
    if norm:
        ms = jnp.mean(x * x, axis=-1, keepdims=True)
        x = (x * lax.rsqrt(ms + RMS_EPS)) * g_ref[...]
    o_ref[...] = x


def _add_norm(a, bt, gain, *, norm, bm=256):
    T, D = a.shape
    spec = pl.BlockSpec((bm, D), lambda i: (i, 0))
    return pl.pallas_call(
        functools.partial(_add_norm_kernel, norm=norm),
        grid=(T // bm,),
        in_specs=[spec, pl.BlockSpec((D, bm), lambda i: (0, i)), pl.BlockSpec((1, D), lambda i: (0, 0))],
        out_specs=spec,
        out_shape=jax.ShapeDtypeStruct((T, D), F32),
        compiler_params=_params(("arbitrary",)),
        name="add_norm",
    )(a, bt, gain.reshape(1, D).astype(F32))


def _rope_tables(seq):
    half = HEAD_DIM // 2
    inv = ROPE_THETA ** (-jnp.arange(half, dtype=F32) / half)
    ang = jnp.arange(seq, dtype=jnp.int32).astype(F32)[:, None] * inv[None, :]
    cos, sin = jnp.cos(ang), jnp.sin(ang)
    return jnp.concatenate([cos, cos], axis=-1), jnp.concatenate([-sin, sin], axis=-1)


def kernel(x, mem, norm_mix, w_in, swa_sink, nat_rel_bias, out_norm_swa, out_norm_nat, w_out, norm_mem_q,
           norm_mem_kv, w_mem_q, w_mem_k, w_mem_v, w_mem_o, norm_ffn, w_peer_q, peer_sub_keys, peer_u, peer_v,
           norm_final):
    B, S, D = x.shape
    M = mem.shape[1]
    T = B * S
    depth = w_in.shape[0]
    assert depth >= 1
    rope = _rope_tables(S)
    h = x.reshape(T, D)
    mem2 = mem.reshape(B * M, D)
    rope_cols = SWA_Q_W + SWA_KV_W
    for l in range(depth):
        qkv = _rms_matmul([h], [norm_mix[l]], w_in[l].astype(BF16), bm=512, bn=1024, out_dtype=BF16,
                          rope=rope, rope_cols=rope_cols, name="in_proj")
        oa = _swa_attention(qkv, swa_sink[l], B, S)
        ob = _nat_attention(qkv, nat_rel_bias[l], B, S)
        h = _rms_matmul([oa, ob], [out_norm_swa[l], out_norm_nat[l]], w_out[l].astype(BF16), bm=512, bn=1024,
                        out_dtype=F32, res=h, name="out_proj")
        w_kv = jnp.concatenate([w_mem_k[l], w_mem_v[l]], axis=1).astype(BF16)
        kv = _rms_matmul([mem2], [norm_mem_kv[l]], w_kv, bm=512, bn=512, out_dtype=BF16, name="mem_kv_proj")
        qm = _rms_matmul([h], [norm_mem_q[l]], w_mem_q[l].astype(BF16), bm=512, bn=MEM_W, out_dtype=BF16,
                         name="mem_q_proj")
        om = _mem_attention(qm, kv, B, S, M)
        h = _rms_matmul([om], None, w_mem_o[l].astype(BF16), bm=512, bn=D, out_dtype=F32, res=h,
                        name="mem_o_proj")
        qp, xn = _rms_matmul([h], [norm_ffn[l]], w_peer_q[l].astype(BF16), bm=512, bn=PEER_HEADS * PEER_QDIM,
                             out_dtype=BF16, emit_xn=True, name="peer_q_proj")
        n1, e1, r2, e2 = _peer_route(qp, peer_sub_keys[l].astype(BF16))
        eb = 512
        vt = _transpose_cast_tiles(peer_v[l], eb)
        peer = _peer_mix(xn.T, peer_u[l].astype(BF16), vt, n1, e1, r2, e2)
        last = l == depth - 1
        h = _add_norm(h, peer, norm_final if last else jnp.ones((D,), F32), norm=last)
    return h.reshape(B, S, D)
```

```python
import functools

import numpy as np
import jax
import jax.numpy as jnp
from jax import lax
from jax.experimental import pallas as pl
from jax.experimental.pallas import tpu as pltpu

F32 = jnp.float32
BF16 = jnp.bfloat16

D_MODEL = 4096
HEAD_DIM = 128
SWA_Q_HEADS = 16
SWA_KV_HEADS = 4
SWA_GROUP = SWA_Q_HEADS // SWA_KV_HEADS
SWA_WINDOW = 128
SWA_BLOCK = 128
ROPE_THETA = 10000.0
NAT_HEADS = 16
NAT_KR_MAX = 8
NAT_KC = 16
GRID_W = 64
NAT_QROWS = 2
SWA_Q_W = SWA_Q_HEADS * HEAD_DIM
SWA_KV_W = SWA_KV_HEADS * HEAD_DIM
NAT_W = NAT_HEADS * HEAD_DIM
MEM_HEADS = 4
MEM_HEAD_DIM = 256
MEM_W = MEM_HEADS * MEM_HEAD_DIM
PEER_HEADS = 8
PEER_NKEYS = 128
PEER_QDIM = 256
PEER_TOPK = 16
RMS_EPS = 1e-6
NEG_INF = -1e30

LANES = 128
VMEM_LIMIT_BYTES = 56 * 1024 * 1024

TILES = {
    "in_proj": dict(bm=512, bn=1024),
    "out_proj": dict(bm=1024, bn=512),
    "mem_kv_proj": dict(bm=512, bn=512),
    "mem_q_proj": dict(bm=512, bn=MEM_W),
    "mem_o_proj": dict(bm=512, bn=D_MODEL),
    "peer_q_proj": dict(bm=512, bn=PEER_HEADS * PEER_QDIM),
}
PEER_EXPERT_TILE = 512

_CELLS = [(a, b) for a in range(PEER_TOPK) for b in range(PEER_TOPK) if (a + 1) * (b + 1) <= PEER_TOPK]
_N_CELLS = len(_CELLS)
_CELL_ROWS = -(-_N_CELLS // 8) * 8


def _params(sem, vmem_limit_bytes=None):
    return pltpu.CompilerParams(dimension_semantics=sem, vmem_limit_bytes=vmem_limit_bytes or VMEM_LIMIT_BYTES)


def _rms_matmul_kernel(*refs, seg_widths, norm, has_res, rope_cols, emit_xn):
    n_seg = len(seg_widths)
    pos = 0
    x_refs = refs[pos:pos + n_seg]; pos += n_seg
    g_refs = ()
    if norm:
        g_refs = refs[pos:pos + n_seg]; pos += n_seg
    w_ref = refs[pos]; pos += 1
    res_ref = None
    if has_res:
        res_ref = refs[pos]; pos += 1
    cos_ref = sin_ref = None
    if rope_cols:
        cos_ref, sin_ref = refs[pos], refs[pos + 1]; pos += 2
    o_ref = refs[pos]; pos += 1
    xn_out_ref = None
    if emit_xn:
        xn_out_ref = refs[pos]; pos += 1
    xn_ref = refs[pos]

    j = pl.program_id(1)

    @pl.when(j == 0)
    def _():
        off = 0
        for s in range(n_seg):
            x = x_refs[s][...].astype(F32)
            if norm:
                ms = jnp.mean(x * x, axis=-1, keepdims=True)
                x = (x * lax.rsqrt(ms + RMS_EPS)) * g_refs[s][...]
            xn_ref[:, off:off + seg_widths[s]] = x.astype(BF16)
            off += seg_widths[s]
        if emit_xn:
            xn_out_ref[...] = xn_ref[...]

    acc = jnp.dot(xn_ref[...], w_ref[...], preferred_element_type=F32)
    if has_res:
        acc = acc + res_ref[...]

    bn = acc.shape[1]
    if rope_cols:
        def store(n_roped_heads):
            for g in range(n_roped_heads):
                sl = slice(g * HEAD_DIM, (g + 1) * HEAD_DIM)
                rot = pltpu.roll(acc[:, sl], HEAD_DIM // 2, 1)
                o_ref[:, sl] = (acc[:, sl] * cos_ref[...] + rot * sin_ref[...]).astype(o_ref.dtype)
            if n_roped_heads * HEAD_DIM < bn:
                o_ref[:, n_roped_heads * HEAD_DIM:] = acc[:, n_roped_heads * HEAD_DIM:].astype(o_ref.dtype)

        full_blocks, rem = divmod(rope_cols, bn)
        pl.when(j < full_blocks)(lambda: store(bn // HEAD_DIM))
        if rem:
            pl.when(j == full_blocks)(lambda: store(rem // HEAD_DIM))
        pl.when(j >= full_blocks + (1 if rem else 0))(lambda: store(0))
    else:
        o_ref[...] = acc.astype(o_ref.dtype)


def _rms_matmul(xs, gains, w, *, bm, bn, out_dtype, res=None, rope=None, rope_cols=0, emit_xn=False, name):
    T = xs[0].shape[0]
    seg_widths = tuple(int(x.shape[1]) for x in xs)
    K = sum(seg_widths)
    N = w.shape[1]
    norm = gains is not None
    assert w.shape[0] == K and T % bm == 0 and N % bn == 0 and rope_cols % HEAD_DIM == 0
    in_specs = [pl.BlockSpec((bm, kw), lambda i, j: (i, 0)) for kw in seg_widths]
    args = list(xs)
    if norm:
        in_specs += [pl.BlockSpec((1, kw), lambda i, j: (0, 0)) for kw in seg_widths]
        args += [g.reshape(1, -1).astype(F32) for g in gains]
    in_specs.append(pl.BlockSpec((K, bn), lambda i, j: (0, j), pipeline_mode=pl.Buffered(1) if bn == N else None))
    args.append(w)
    if res is not None:
        in_specs.append(pl.BlockSpec((bm, bn), lambda i, j: (i, j)))
        args.append(res)
    if rope_cols:
        cos, sin = rope
        pos_blocks = cos.shape[0] // bm
        in_specs += [pl.BlockSpec((bm, HEAD_DIM), lambda i, j: (i % pos_blocks, 0))] * 2
        args += [cos, sin]
    out_shape = [jax.ShapeDtypeStruct((T, N), out_dtype)]
    out_specs = [pl.BlockSpec((bm, bn), lambda i, j: (i, j))]
    if emit_xn:
        out_shape.append(jax.ShapeDtypeStruct((T, K), BF16))
        out_specs.append(pl.BlockSpec((bm, K), lambda i, j: (i, 0)))
    kern = functools.partial(_rms_matmul_kernel, seg_widths=seg_widths, norm=norm, has_res=res is not None,
                             rope_cols=rope_cols, emit_xn=emit_xn)
    outs = pl.pallas_call(
        kern,
        grid=(T // bm, N // bn),
        in_specs=in_specs,
        out_specs=out_specs,
        out_shape=out_shape,
        scratch_shapes=[pltpu.VMEM((bm, K), BF16)],
        compiler_params=_params(("arbitrary", "arbitrary")),
        name=name,
    )(*args)
    return outs if emit_xn else outs[0]


def _swa_kernel(q_ref, k_ref, v_ref, sink_ref, o_ref, *, seq, qblocks):
    band = 3 * SWA_BLOCK
    scale = HEAD_DIM ** -0.5
    sink = sink_ref[0]
    blocks = range(qblocks)
    rows = [slice(t * SWA_BLOCK, (t + 1) * SWA_BLOCK) for t in blocks]
    scores, values = [], []
    for t in blocks:
        n = pl.program_id(2) * qblocks + t
        start = pl.multiple_of(jnp.clip((n - 1) * SWA_BLOCK, 0, seq - band), SWA_BLOCK)
        k = k_ref[pl.ds(start, band), :]
        values.append(v_ref[pl.ds(start, band), :].astype(F32).T.astype(BF16))
        kpos = start + lax.broadcasted_iota(jnp.int32, (band, SWA_BLOCK), 0)
        qpos = n * SWA_BLOCK + lax.broadcasted_iota(jnp.int32, (band, SWA_BLOCK), 1)
        valid = jnp.abs(qpos - kpos) <= SWA_WINDOW
        q = jnp.concatenate([q_ref[rows[t], g * HEAD_DIM:(g + 1) * HEAD_DIM] for g in range(SWA_GROUP)], axis=0)
        s = lax.dot_general(k, q, (((1,), (1,)), ((), ())), preferred_element_type=F32) * scale
        scores.append(jnp.where(jnp.concatenate([valid] * SWA_GROUP, axis=1), s, NEG_INF))
    maxes = [jnp.maximum(jnp.max(s, axis=0, keepdims=True), sink) for s in scores]
    probs = [jnp.exp(s - m) for s, m in zip(scores, maxes)]
    denoms = [jnp.sum(p, axis=0, keepdims=True) + jnp.exp(sink - m) for p, m in zip(probs, maxes)]
    outs = [jnp.dot(vt, (p * (1.0 / d)).astype(BF16), preferred_element_type=F32)
            for p, d, vt in zip(probs, denoms, values)]
    for t in blocks:
        for g in range(SWA_GROUP):
            o_ref[rows[t], g * HEAD_DIM:(g + 1) * HEAD_DIM] = (
                outs[t][:, g * SWA_BLOCK:(g + 1) * SWA_BLOCK].T.astype(o_ref.dtype))


def _swa_attention(qkv, sink, batch, seq, *, qblocks=4):
    T = batch * seq
    nb = seq // (SWA_BLOCK * qblocks)
    gw = SWA_GROUP * HEAD_DIM
    k_col0 = SWA_Q_W // HEAD_DIM
    v_col0 = (SWA_Q_W + SWA_KV_W) // HEAD_DIM
    sink_b = jnp.broadcast_to(sink.astype(F32).reshape(SWA_KV_HEADS, 1, SWA_GROUP, 1),
                              (SWA_KV_HEADS, 1, SWA_GROUP, SWA_BLOCK)).reshape(SWA_KV_HEADS, 1, gw)
    return pl.pallas_call(
        functools.partial(_swa_kernel, seq=seq, qblocks=qblocks),
        grid=(batch, SWA_KV_HEADS, nb),
        in_specs=[
            pl.BlockSpec((SWA_BLOCK * qblocks, gw), lambda b, h, n: (b * nb + n, h)),
            pl.BlockSpec((seq, HEAD_DIM), lambda b, h, n: (b, k_col0 + h)),
            pl.BlockSpec((seq, HEAD_DIM), lambda b, h, n: (b, v_col0 + h)),
            pl.BlockSpec((1, 1, gw), lambda b, h, n: (h, 0, 0)),
        ],
        out_specs=pl.BlockSpec((SWA_BLOCK * qblocks, gw), lambda b, h, n: (b * nb + n, h)),
        out_shape=jax.ShapeDtypeStruct((T, SWA_Q_W), BF16),
        compiler_params=_params(("arbitrary", "arbitrary", "arbitrary")),
        name="swa_attention",
    )(qkv, qkv, qkv, sink_b)


def _nat_tables(seq):
    rows = seq // GRID_W
    kr = min(NAT_KR_MAX, rows)
    span = min(rows, NAT_QROWS + kr - 1)
    nb = rows // NAT_QROWS
    qbl = NAT_QROWS * GRID_W
    r0 = np.arange(nb) * NAT_QROWS
    kstart = np.clip(r0 - kr // 2, 0, rows - span)
    key_rows = kstart[:, None] + np.arange(span)[None, :]
    q_row = r0[:, None] + (np.arange(qbl) // GRID_W)[None, :]
    q_col = np.broadcast_to((np.arange(qbl) % GRID_W)[None, :], q_row.shape)
    k_row = np.repeat(key_rows, GRID_W, axis=1)
    k_col = np.broadcast_to(np.tile(np.arange(GRID_W), span)[None, :], k_row.shape)
    rs = np.clip(q_row - kr // 2, 0, rows - kr)[:, :, None]
    cs = np.clip(q_col - NAT_KC // 2, 0, GRID_W - NAT_KC)[:, :, None]
    kr3, kc3 = k_row[:, None, :], k_col[:, None, :]
    valid = (kr3 >= rs) & (kr3 < rs + kr) & (kc3 >= cs) & (kc3 < cs + NAT_KC)
    dr = np.clip(kr3 - q_row[:, :, None] + NAT_KR_MAX - 1, 0, 2 * NAT_KR_MAX - 2)
    dc = np.clip(kc3 - q_col[:, :, None] + NAT_KC - 1, 0, 2 * NAT_KC - 2)
    flat = np.where(valid, dr * (2 * NAT_KC - 1) + dc, -1).reshape(nb, -1)
    _, first, inverse = np.unique(flat, axis=0, return_index=True, return_inverse=True)
    valid, dr, dc = valid[first], dr[first], dc[first]
    dr_tab = dr[:, ::GRID_W, ::GRID_W]
    assert (dr == np.repeat(np.repeat(dr_tab, GRID_W, axis=1), GRID_W, axis=2)).all()
    col = np.arange(GRID_W)
    toeplitz = np.clip(col[None, :] - col[:, None] + NAT_KC - 1, 0, 2 * NAT_KC - 2)
    assert (dc == np.tile(toeplitz, (NAT_QROWS, span))[None]).all()
    far = np.tile(np.abs(col[None, :] - col[:, None]) >= NAT_KC, (NAT_QROWS, span))
    assert not (valid & far[None]).any()
    return dict(nb=nb, qbl=qbl, span=span, nkeys=span * GRID_W, start=(kstart * GRID_W).astype(np.int32),
                pid=np.asarray(inverse).reshape(-1).astype(np.int32), valid=valid, dr_tab=dr_tab)


def _nat_bias_kernel(rows_ref, valid_ref, o_ref, *, span):
    n_pairs = -(-span // 2)
    for h in range(o_ref.shape[1]):
        for qr in range(NAT_QROWS):
            rs = slice(qr * GRID_W, (qr + 1) * GRID_W)
            for pair in range(n_pairs):
                row = rows_ref[0, h, qr * n_pairs + pair:qr * n_pairs + pair + 1, :]
                blk = pltpu.roll(jnp.broadcast_to(row, (GRID_W, LANES)), LANES - (NAT_KC - 1), 1,
                                 stride=1, stride_axis=0)
                width = min(LANES, span * GRID_W - pair * LANES)
                cs = slice(pair * LANES, pair * LANES + width)
                o_ref[0, h, rs, cs] = jnp.where(valid_ref[0, rs, cs] > 0, blk[:, :width], NEG_INF)


def _nat_bias_table(rel_bias, tab):
    span, qbl, nkeys = tab["span"], tab["qbl"], tab["nkeys"]
    n_pat = tab["valid"].shape[0]
    n_pairs = -(-span // 2)
    nrel = 2 * NAT_KC - 1
    n_rows = -(-NAT_QROWS * n_pairs // 8) * 8
    assert 2 * GRID_W == LANES and nrel <= GRID_W
    rows = rel_bias.astype(F32)[:, tab["dr_tab"]]
    rows = jnp.pad(rows, ((0, 0), (0, 0), (0, 0), (0, 2 * n_pairs - span), (0, GRID_W - nrel)))
    rows = jnp.transpose(rows.reshape(NAT_HEADS, n_pat, NAT_QROWS * n_pairs, LANES), (1, 0, 2, 3))
    rows = jnp.pad(rows, ((0, 0), (0, 0), (0, n_rows - NAT_QROWS * n_pairs), (0, 0)))
    valid = jnp.asarray(tab["valid"].astype(np.float32))
    return pl.pallas_call(
        functools.partial(_nat_bias_kernel, span=span),
        grid=(n_pat,),
        in_specs=[
            pl.BlockSpec((1, NAT_HEADS, n_rows, LANES), lambda p: (p, 0, 0, 0)),
            pl.BlockSpec((1, qbl, nkeys), lambda p: (p, 0, 0)),
        ],
        out_specs=pl.BlockSpec((1, NAT_HEADS, qbl, nkeys), lambda p: (p, 0, 0, 0)),
        out_shape=jax.ShapeDtypeStruct((n_pat, NAT_HEADS, qbl, nkeys), F32),
        compiler_params=_params(("arbitrary",)),
        name="nat_bias_table",
    )(rows, valid)


def _nat_kernel(pid_ref, start_ref, q_ref, k_ref, v_ref, *rest, nkeys, heads, qbl):
    del pid_ref
    bias_refs, o_ref = rest[:-1], rest[-1]
    scale = HEAD_DIM ** -0.5
    sls = [slice(h * HEAD_DIM, (h + 1) * HEAD_DIM) for h in range(heads)]
    blocks = range(len(bias_refs))
    rows = [slice(t * qbl, (t + 1) * qbl) for t in blocks]
    scores, values = [], []
    for t in blocks:
        start = pl.multiple_of(start_ref[pl.program_id(2) * len(bias_refs) + t], GRID_W)
        q = jnp.stack([q_ref[rows[t], sl] for sl in sls], axis=0)
        k = jnp.stack([k_ref[pl.ds(start, nkeys), sl] for sl in sls], axis=0)
        values.append(jnp.stack([v_ref[pl.ds(start, nkeys), sl] for sl in sls], axis=0))
        s = lax.dot_general(q, k, (((2,), (2,)), ((0,), (0,))), preferred_element_type=F32) * scale
        scores.append(s + bias_refs[t][0])
    maxes = [jnp.max(s, axis=-1, keepdims=True) for s in scores]
    probs = [jnp.exp(s - m) for s, m in zip(scores, maxes)]
    probs = [(p * (1.0 / jnp.sum(p, axis=-1, keepdims=True))).astype(BF16) for p in probs]
    outs = [lax.dot_general(p, v, (((2,), (1,)), ((0,), (0,))), preferred_element_type=F32)
            for p, v in zip(probs, values)]
    for t in blocks:
        for h, sl in enumerate(sls):
            o_ref[rows[t], sl] = outs[t][h].astype(o_ref.dtype)


def _nat_attention(qkv, rel_bias, batch, seq, *, qblocks=4):
    T = batch * seq
    tab = _nat_tables(seq)
    qbl, nkeys = tab["qbl"], tab["nkeys"]
    nb = tab["nb"] // qblocks
    bias = _nat_bias_table(rel_bias, tab)
    hpg = 4
    gw = hpg * HEAD_DIM
    n_hg = NAT_HEADS // hpg
    q0 = (SWA_Q_W + 2 * SWA_KV_W) // gw
    k0 = q0 + n_hg
    v0 = k0 + n_hg

    def bias_spec(t):
        return pl.BlockSpec((1, hpg, qbl, nkeys), lambda b, g, n, pid, st: (pid[n * qblocks + t], g, 0, 0))

    grid_spec = pltpu.PrefetchScalarGridSpec(
        num_scalar_prefetch=2,
        grid=(batch, n_hg, nb),
        in_specs=[
            pl.BlockSpec((qbl * qblocks, gw), lambda b, g, n, pid, st: (b * nb + n, q0 + g)),
            pl.BlockSpec((seq, gw), lambda b, g, n, pid, st: (b, k0 + g)),
            pl.BlockSpec((seq, gw), lambda b, g, n, pid, st: (b, v0 + g)),
        ] + [bias_spec(t) for t in range(qblocks)],
        out_specs=pl.BlockSpec((qbl * qblocks, gw), lambda b, g, n, pid, st: (b * nb + n, g)),
    )
    return pl.pallas_call(
        functools.partial(_nat_kernel, nkeys=nkeys, heads=hpg, qbl=qbl),
        grid_spec=grid_spec,
        out_shape=jax.ShapeDtypeStruct((T, NAT_W), BF16),
        compiler_params=_params(("arbitrary", "arbitrary", "arbitrary")),
        name="nat_attention",
    )(jnp.asarray(tab["pid"]), jnp.asarray(tab["start"]), qkv, qkv, qkv, *([bias] * qblocks))


def _mem_attn_kernel(q_ref, kv_ref, o_ref):
    scale = MEM_HEAD_DIM ** -0.5
    for h in range(MEM_HEADS):
        sl = slice(h * MEM_HEAD_DIM, (h + 1) * MEM_HEAD_DIM)
        q = q_ref[:, sl]
        k = kv_ref[:, sl]
        v = kv_ref[:, MEM_W + h * MEM_HEAD_DIM:MEM_W + (h + 1) * MEM_HEAD_DIM]
        s = lax.dot_general(q, k, (((1,), (1,)), ((), ())), preferred_element_type=F32) * scale
        m = jnp.max(s, axis=-1, keepdims=True)
        p = jnp.exp(s - m)
        p = (p / jnp.sum(p, axis=-1, keepdims=True)).astype(BF16)
        o_ref[:, sl] = jnp.dot(p, v, preferred_element_type=F32).astype(o_ref.dtype)


def _mem_attention(q, kv, batch, seq, mem_len, *, bq=512):
    T = batch * seq
    nq = seq // bq
    return pl.pallas_call(
        _mem_attn_kernel,
        grid=(batch, nq),
        in_specs=[
            pl.BlockSpec((bq, MEM_W), lambda b, i: (b * nq + i, 0)),
            pl.BlockSpec((mem_len, 2 * MEM_W), lambda b, i: (b, 0)),
        ],
        out_specs=pl.BlockSpec((bq, MEM_W), lambda b, i: (b * nq + i, 0)),
        out_shape=jax.ShapeDtypeStruct((T, MEM_W), BF16),
        compiler_params=_params(("arbitrary", "arbitrary")),
        name="mem_attention",
    )(q, kv)


def _extract_max(work, rows):
    m = jnp.max(work, axis=0, keepdims=True)
    first = jnp.min(jnp.where(work == m, rows, float(work.shape[0])), axis=0, keepdims=True)
    return m, rows == first


def _top16_rows(scores, dests, work_scr, rank_scr, base):
    shape = scores[0].shape
    rows = lax.broadcasted_iota(jnp.int32, shape, 0).astype(F32)
    chains = range(len(scores))

    def run(first_only):
        for n, s in enumerate(scores):
            work_scr[base + n] = s
            rank_scr[base + n] = jnp.full(shape, float(PEER_TOPK), F32)

        def body(a, carry):
            for n in chains:
                work = work_scr[base + n]
                if first_only:
                    m, hit = _extract_max(work, rows)
                else:
                    m = jnp.max(work, axis=0, keepdims=True)
                    hit = work == m
                ref, idx = dests[n]
                ref[idx, pl.ds(a, 1), :] = m
                work_scr[base + n] = jnp.where(hit, -jnp.inf, work)
                rank_scr[base + n] = jnp.where(hit, jnp.asarray(a, F32), rank_scr[base + n])
            return carry

        lax.fori_loop(0, PEER_TOPK, body, 0)

    run(first_only=False)
    excess = jnp.zeros((1, shape[1]), F32)
    for n in chains:
        taken = jnp.sum(jnp.where(rank_scr[base + n] < float(PEER_TOPK), 1.0, 0.0), axis=0, keepdims=True)
        excess = jnp.maximum(excess, jnp.abs(taken - float(PEER_TOPK)))
    pl.when(jnp.max(excess) > 0.0)(lambda: run(first_only=True))
    return [rank_scr[base + n] for n in chains]


def _route_kernel(q_ref, keys_ref, n1_ref, e1_ref, r2_ref, e2_ref, v1_scr, v2_scr, c_scr, sel_scr, work_scr, rank_scr,
                  *, tb, group):
    half = PEER_QDIM // 2
    dn = (((1,), (1,)), ((), ()))
    s1_all = lax.dot_general(keys_ref[0, 0], q_ref[:, :half], dn, preferred_element_type=F32)
    s2_all = lax.dot_general(keys_ref[0, 1], q_ref[:, half:], dn, preferred_element_type=F32)
    n_c = tb // LANES
    cols = [slice(c * LANES, (c + 1) * LANES) for c in range(n_c)]
    s1 = [s1_all[:, cl] for cl in cols]
    s2 = [s2_all[:, cl] for cl in cols]
    rank1, rank2 = [], []
    for c0 in range(0, n_c, group):
        cs = list(range(c0, c0 + group))
        ranks = _top16_rows([s1[c] for c in cs] + [s2[c] for c in cs],
                            [(v1_scr, c) for c in cs] + [(v2_scr, c) for c in cs], work_scr, rank_scr, 2 * c0)
        rank1 += ranks[:group]
        rank2 += ranks[group:]
    for c in range(n_c):
        for p, (a, b) in enumerate(_CELLS):
            c_scr[c, p:p + 1, :] = v1_scr[c, a:a + 1, :] + v2_scr[c, b:b + 1, :]
        if _CELL_ROWS > _N_CELLS:
            c_scr[c, _N_CELLS:, :] = jnp.full((_CELL_ROWS - _N_CELLS, LANES), -jnp.inf, F32)
    cand = [c_scr[c] for c in range(n_c)]
    prow = lax.broadcasted_iota(jnp.int32, cand[0].shape, 0).astype(F32)

    def body(t, carry):
        out = []
        for work, picked in carry:
            _, hit = _extract_max(work, prow)
            out.append((jnp.where(hit, -jnp.inf, work), jnp.where(hit, 1.0, picked)))
        return tuple(out)

    picked = [pk for _, pk in lax.fori_loop(0, PEER_TOPK, body,
                                            tuple((cd, jnp.zeros(cd.shape, F32)) for cd in cand))]
    for c in range(n_c):
        top1 = v1_scr[c, 0:1, :]
        top2 = v2_scr[c, 0:1, :]
        z = jnp.sum(jnp.where(picked[c] > 0, jnp.exp(cand[c] - (top1 + top2)), 0.0), axis=0, keepdims=True)
        sel_scr[c] = picked[c]
        n1 = jnp.zeros(s1[c].shape, F32)
        p0 = 0
        for a in range(PEER_TOPK):
            width = sum(1 for (aa, _) in _CELLS if aa == a)
            n_a = jnp.sum(sel_scr[c, p0:p0 + width, :], axis=0, keepdims=True)
            n1 = jnp.where(rank1[c] == float(a), n_a, n1)
            p0 += width
        n1_ref[0, :, cols[c]] = n1
        e1_ref[0, :, cols[c]] = jnp.exp(s1[c] - top1) / z
        r2_ref[0, :, cols[c]] = rank2[c].astype(r2_ref.dtype)
        e2_ref[0, :, cols[c]] = jnp.exp(s2[c] - top2).astype(e2_ref.dtype)


def _peer_route(q, sub_keys, *, tb=512, group=4):
    T = q.shape[0]
    shape = jax.ShapeDtypeStruct((PEER_HEADS, PEER_NKEYS, T), F32)
    shape_packed = jax.ShapeDtypeStruct((PEER_HEADS, PEER_NKEYS, T), BF16)
    out_spec = pl.BlockSpec((1, PEER_NKEYS, tb), lambda i, h: (h, 0, i))
    return pl.pallas_call(
        functools.partial(_route_kernel, tb=tb, group=group),
        grid=(T // tb, PEER_HEADS),
        in_specs=[
            pl.BlockSpec((tb, PEER_QDIM), lambda i, h: (i, h)),
            pl.BlockSpec((1, 2, PEER_NKEYS, PEER_QDIM // 2), lambda i, h: (h, 0, 0, 0)),
        ],
        out_specs=[out_spec] * 4,
        out_shape=[shape, shape, shape_packed, shape_packed],
        scratch_shapes=[pltpu.VMEM((tb // LANES, PEER_TOPK, LANES), F32),
                        pltpu.VMEM((tb // LANES, PEER_TOPK, LANES), F32),
                        pltpu.VMEM((tb // LANES, _CELL_ROWS, LANES), F32),
                        pltpu.VMEM((tb // LANES, _CELL_ROWS, LANES), F32),
                        pltpu.VMEM((2 * (tb // LANES), PEER_NKEYS, LANES), F32),
                        pltpu.VMEM((2 * (tb // LANES), PEER_NKEYS, LANES), F32)],
        compiler_params=_params(("arbitrary", "arbitrary")),
        name="peer_route",
    )(q, sub_keys)


def _peer_kernel(xnt_ref, u_ref, vt_ref, n1_ref, e1_ref, r2_ref, e2_ref, o_ref, a_scr, w_scr, *, eb, tc):
    k = pl.program_id(1)

    @pl.when(k == 0)
    def _():
        o_ref[...] = jnp.zeros(o_ref.shape, F32)
        a_scr[...] = jnp.dot(u_ref[...], xnt_ref[...], preferred_element_type=F32)

    @pl.when(k > 0)
    def _():
        n_i = eb // PEER_NKEYS
        pack = 16
        groups = PEER_NKEYS // pack
        for c in range(a_scr.shape[1] // tc):
            cols = slice(c * tc, (c + 1) * tc)
            for a in range(n_i):
                rows = slice(a * PEER_NKEYS, (a + 1) * PEER_NKEYS)
                i_row = k * n_i + (a - n_i)
                a_t = a_scr[rows, cols]
                act = 0.5 * a_t * (1.0 + lax.erf(a_t * np.float32(np.sqrt(0.5))))
                gate = jnp.zeros((PEER_NKEYS, tc), BF16)
                for h in range(PEER_HEADS):
                    cnt = jnp.broadcast_to(n1_ref[h, pl.ds(i_row, 1), :][:, cols], (pack, tc)).astype(BF16)
                    g1 = jnp.broadcast_to(e1_ref[h, pl.ds(i_row, 1), :][:, cols], (pack, tc)).astype(BF16)
                    cnt = jnp.concatenate([cnt] * groups, axis=0)
                    g1 = jnp.concatenate([g1] * groups, axis=0)
                    gate = gate + jnp.where(r2_ref[h, :, cols] < cnt, e2_ref[h, :, cols] * g1,
                                            jnp.zeros((), BF16))
                w_scr[rows, cols] = (act * gate.astype(F32)).astype(BF16)
        a_scr[...] = jnp.dot(u_ref[...], xnt_ref[...], preferred_element_type=F32)
        o_ref[...] += jnp.dot(vt_ref[0], w_scr[...], preferred_element_type=F32)


def _peer_mix(xnt, u, vt, n1, e1, r2, e2, *, tb=512, tc=512):
    D, T = xnt.shape
    E = u.shape[0]
    n_e, _, eb = vt.shape
    once = pl.Buffered(1)
    route_spec = pl.BlockSpec((PEER_HEADS, PEER_NKEYS, tb), lambda i, k: (0, 0, i), pipeline_mode=once)
    return pl.pallas_call(
        functools.partial(_peer_kernel, eb=eb, tc=tc),
        grid=(T // tb, n_e + 1),
        in_specs=[
            pl.BlockSpec((D, tb), lambda i, k: (0, i), pipeline_mode=once),
            pl.BlockSpec((eb, D), lambda i, k: (jnp.minimum(k, n_e - 1), 0)),
            pl.BlockSpec((1, D, eb), lambda i, k: (jnp.maximum(k - 1, 0), 0, 0)),
            route_spec, route_spec, route_spec, route_spec,
        ],
        out_specs=pl.BlockSpec((D, tb), lambda i, k: (0, i)),
        out_shape=jax.ShapeDtypeStruct((D, T), F32),
        scratch_shapes=[pltpu.VMEM((eb, tb), F32), pltpu.VMEM((eb, tb), BF16)],
        compiler_params=_params(("arbitrary", "arbitrary")),
        name="peer_mix",
    )(xnt, u, vt, n1, e1, r2, e2)


def _transpose_cast_kernel(x_ref, o_ref):
    o_ref[0] = x_ref[...].T.astype(o_ref.dtype)


def _transpose_cast_tiles(x, rows):
    N, D = x.shape
    return pl.pallas_call(
        _transpose_cast_kernel,
        grid=(N // rows,),
        in_specs=[pl.BlockSpec((rows, D), lambda e: (e, 0))],
        out_specs=pl.BlockSpec((1, D, rows), lambda e: (e, 0, 0)),
        out_shape=jax.ShapeDtypeStruct((N // rows, D, rows), BF16),
        compiler_params=_params(("arbitrary",)),
        name="transpose_cast",
    )(x)


def _add_norm_kernel(a_ref, bt_ref, g_ref, o_ref, *, norm):
    x = a_ref[...] + bt_ref[...].T
    if norm:
        ms = jnp.mean(x * x, axis=-1, keepdims=True)
        x = (x * lax.rsqrt(ms + RMS_EPS)) * g_ref[...]
    o_ref[...] = x


def _add_norm(a, bt, gain, *, norm, bm=256):
    T, D = a.shape
    spec = pl.BlockSpec((bm, D), lambda i: (i, 0))
    return pl.pallas_call(
        functools.partial(_add_norm_kernel, norm=norm),
        grid=(T // bm,),
        in_specs=[spec, pl.BlockSpec((D, bm), lambda i: (0, i)), pl.BlockSpec((1, D), lambda i: (0, 0))],
        out_specs=spec,
        out_shape=jax.ShapeDtypeStruct((T, D), F32),
        compiler_params=_params(("arbitrary",)),
        name="add_norm",
    )(a, bt, gain.reshape(1, D).astype(F32))


def _rope_tables(seq):
    half = HEAD_DIM // 2
    inv = ROPE_THETA ** (-jnp.arange(half, dtype=F32) / half)
    ang = jnp.arange(seq, dtype=jnp.int32).astype(F32)[:, None] * inv[None, :]
    cos, sin = jnp.cos(ang), jnp.sin(ang)
    return jnp.concatenate([cos, cos], axis=-1), jnp.concatenate([-sin, sin], axis=-1)


def kernel(x, mem, norm_mix, w_in, swa_sink, nat_rel_bias, out_norm_swa, out_norm_nat, w_out, norm_mem_q,
           norm_mem_kv, w_mem_q, w_mem_k, w_mem_v, w_mem_o, norm_ffn, w_peer_q, peer_sub_keys, peer_u, peer_v,
           norm_final):
    B, S, D = x.shape
    M = mem.shape[1]
    T = B * S
    depth = w_in.shape[0]
    assert depth >= 1
    rope = _rope_tables(S)
    h = x.reshape(T, D)
    mem2 = mem.reshape(B * M, D)
    rope_cols = SWA_Q_W + SWA_KV_W
    for l in range(depth):
        qkv = _rms_matmul([h], [norm_mix[l]], w_in[l].astype(BF16), **TILES["in_proj"], out_dtype=BF16,
                          rope=rope, rope_cols=rope_cols, name="in_proj")
        oa = _swa_attention(qkv, swa_sink[l], B, S)
        ob = _nat_attention(qkv, nat_rel_bias[l], B, S)
        h = _rms_matmul([oa, ob], [out_norm_swa[l], out_norm_nat[l]], w_out[l].astype(BF16), **TILES["out_proj"],
                        out_dtype=F32, res=h, name="out_proj")
        w_kv = jnp.concatenate([w_mem_k[l], w_mem_v[l]], axis=1).astype(BF16)
        kv = _rms_matmul([mem2], [norm_mem_kv[l]], w_kv, **TILES["mem_kv_proj"], out_dtype=BF16,
                         name="mem_kv_proj")
        qm = _rms_matmul([h], [norm_mem_q[l]], w_mem_q[l].astype(BF16), **TILES["mem_q_proj"], out_dtype=BF16,
                         name="mem_q_proj")
        om = _mem_attention(qm, kv, B, S, M)
        h = _rms_matmul([om], None, w_mem_o[l].astype(BF16), **TILES["mem_o_proj"], out_dtype=F32, res=h,
                        name="mem_o_proj")
        qp, xn = _rms_matmul([h], [norm_ffn[l]], w_peer_q[l].astype(BF16), **TILES["peer_q_proj"],
                             out_dtype=BF16, emit_xn=True, name="peer_q_proj")
        n1, e1, r2, e2 = _peer_route(qp, peer_sub_keys[l].astype(BF16))
        vt = _transpose_cast_tiles(peer_v[l], PEER_EXPERT_TILE)
        peer = _peer_mix(xn.T, peer_u[l].astype(BF16), vt, n1, e1, r2, e2)
        last = l == depth - 1
        h = _add_norm(h, peer, norm_final if last else jnp.ones((D,), F32), norm=last)
    return h.reshape(B, S, D)
```

```python
import functools

import numpy as np
import jax
import jax.numpy as jnp
from jax import lax
from jax.experimental import pallas as pl
from jax.experimental.pallas import tpu as pltpu

F32 = jnp.float32
BF16 = jnp.bfloat16

D_MODEL = 4096
HEAD_DIM = 128
SWA_Q_HEADS = 16
SWA_KV_HEADS = 4
SWA_GROUP = SWA_Q_HEADS // SWA_KV_HEADS
SWA_WINDOW = 128
SWA_BLOCK = 128
ROPE_THETA = 10000.0
NAT_HEADS = 16
NAT_KR_MAX = 8
NAT_KC = 16
GRID_W = 64
NAT_QROWS = 2
SWA_Q_W = SWA_Q_HEADS * HEAD_DIM
SWA_KV_W = SWA_KV_HEADS * HEAD_DIM
NAT_W = NAT_HEADS * HEAD_DIM
MEM_HEADS = 4
MEM_HEAD_DIM = 256
MEM_W = MEM_HEADS * MEM_HEAD_DIM
PEER_HEADS = 8
PEER_NKEYS = 128
PEER_QDIM = 256
PEER_TOPK = 16
RMS_EPS = 1e-6
NEG_INF = -1e30

LANES = 128
VMEM_LIMIT_BYTES = 56 * 1024 * 1024

TILES = {
    "in_proj": dict(bm=512, bn=1024),
    "out_proj": dict(bm=1024, bn=512),
    "mem_kv_proj": dict(bm=512, bn=512),
    "peer_q_proj": dict(bm=512, bn=PEER_HEADS * PEER_QDIM),
}
MEM_QUERY_BLOCK = 512
PEER_EXPERT_TILE = 512

_CELLS = [(a, b) for a in range(PEER_TOPK) for b in range(PEER_TOPK) if (a + 1) * (b + 1) <= PEER_TOPK]
_N_CELLS = len(_CELLS)
_CELL_ROWS = -(-_N_CELLS // 8) * 8


def _params(sem, vmem_limit_bytes=None):
    return pltpu.CompilerParams(dimension_semantics=sem, vmem_limit_bytes=vmem_limit_bytes or VMEM_LIMIT_BYTES)


def _rms_matmul_kernel(*refs, seg_widths, norm, has_res, rope_cols, emit_xn):
    n_seg = len(seg_widths)
    pos = 0
    x_refs = refs[pos:pos + n_seg]; pos += n_seg
    g_refs = ()
    if norm:
        g_refs = refs[pos:pos + n_seg]; pos += n_seg
    w_ref = refs[pos]; pos += 1
    res_ref = None
    if has_res:
        res_ref = refs[pos]; pos += 1
    cos_ref = sin_ref = None
    if rope_cols:
        cos_ref, sin_ref = refs[pos], refs[pos + 1]; pos += 2
    o_ref = refs[pos]; pos += 1
    xn_out_ref = None
    if emit_xn:
        xn_out_ref = refs[pos]; pos += 1
    xn_ref = refs[pos]

    j = pl.program_id(1)

    @pl.when(j == 0)
    def _():
        off = 0
        for s in range(n_seg):
            x = x_refs[s][...].astype(F32)
            if norm:
                ms = jnp.mean(x * x, axis=-1, keepdims=True)
                x = (x * lax.rsqrt(ms + RMS_EPS)) * g_refs[s][...]
            xn_ref[:, off:off + seg_widths[s]] = x.astype(BF16)
            off += seg_widths[s]
        if emit_xn:
            xn_out_ref[...] = xn_ref[...]

    acc = jnp.dot(xn_ref[...], w_ref[...], preferred_element_type=F32)
    if has_res:
        acc = acc + res_ref[...]

    bn = acc.shape[1]
    if rope_cols:
        def store(n_roped_heads):
            for g in range(n_roped_heads):
                sl = slice(g * HEAD_DIM, (g + 1) * HEAD_DIM)
                rot = pltpu.roll(acc[:, sl], HEAD_DIM // 2, 1)
                o_ref[:, sl] = (acc[:, sl] * cos_ref[...] + rot * sin_ref[...]).astype(o_ref.dtype)
            if n_roped_heads * HEAD_DIM < bn:
                o_ref[:, n_roped_heads * HEAD_DIM:] = acc[:, n_roped_heads * HEAD_DIM:].astype(o_ref.dtype)

        full_blocks, rem = divmod(rope_cols, bn)
        pl.when(j < full_blocks)(lambda: store(bn // HEAD_DIM))
        if rem:
            pl.when(j == full_blocks)(lambda: store(rem // HEAD_DIM))
        pl.when(j >= full_blocks + (1 if rem else 0))(lambda: store(0))
    else:
        o_ref[...] = acc.astype(o_ref.dtype)


def _rms_matmul(xs, gains, w, *, bm, bn, out_dtype, res=None, rope=None, rope_cols=0, emit_xn=False, name):
    T = xs[0].shape[0]
    seg_widths = tuple(int(x.shape[1]) for x in xs)
    K = sum(seg_widths)
    N = w.shape[1]
    norm = gains is not None
    assert w.shape[0] == K and T % bm == 0 and N % bn == 0 and rope_cols % HEAD_DIM == 0
    in_specs = [pl.BlockSpec((bm, kw), lambda i, j: (i, 0)) for kw in seg_widths]
    args = list(xs)
    if norm:
        in_specs += [pl.BlockSpec((1, kw), lambda i, j: (0, 0)) for kw in seg_widths]
        args += [g.reshape(1, -1).astype(F32) for g in gains]
    in_specs.append(pl.BlockSpec((K, bn), lambda i, j: (0, j), pipeline_mode=pl.Buffered(1) if bn == N else None))
    args.append(w)
    if res is not None:
        in_specs.append(pl.BlockSpec((bm, bn), lambda i, j: (i, j)))
        args.append(res)
    if rope_cols:
        cos, sin = rope
        pos_blocks = cos.shape[0] // bm
        in_specs += [pl.BlockSpec((bm, HEAD_DIM), lambda i, j: (i % pos_blocks, 0))] * 2
        args += [cos, sin]
    out_shape = [jax.ShapeDtypeStruct((T, N), out_dtype)]
    out_specs = [pl.BlockSpec((bm, bn), lambda i, j: (i, j))]
    if emit_xn:
        out_shape.append(jax.ShapeDtypeStruct((T, K), BF16))
        out_specs.append(pl.BlockSpec((bm, K), lambda i, j: (i, 0)))
    kern = functools.partial(_rms_matmul_kernel, seg_widths=seg_widths, norm=norm, has_res=res is not None,
                             rope_cols=rope_cols, emit_xn=emit_xn)
    outs = pl.pallas_call(
        kern,
        grid=(T // bm, N // bn),
        in_specs=in_specs,
        out_specs=out_specs,
        out_shape=out_shape,
        scratch_shapes=[pltpu.VMEM((bm, K), BF16)],
        compiler_params=_params(("arbitrary", "arbitrary")),
        name=name,
    )(*args)
    return outs if emit_xn else outs[0]


def _swa_kernel(q_ref, k_ref, v_ref, sink_ref, o_ref, *, seq, qblocks):
    band = 3 * SWA_BLOCK
    scale = HEAD_DIM ** -0.5
    sink = sink_ref[0]
    blocks = range(qblocks)
    rows = [slice(t * SWA_BLOCK, (t + 1) * SWA_BLOCK) for t in blocks]
    scores, values = [], []
    for t in blocks:
        n = pl.program_id(2) * qblocks + t
        start = pl.multiple_of(jnp.clip((n - 1) * SWA_BLOCK, 0, seq - band), SWA_BLOCK)
        k = k_ref[pl.ds(start, band), :]
        values.append(v_ref[pl.ds(start, band), :].astype(F32).T.astype(BF16))
        kpos = start + lax.broadcasted_iota(jnp.int32, (band, SWA_BLOCK), 0)
        qpos = n * SWA_BLOCK + lax.broadcasted_iota(jnp.int32, (band, SWA_BLOCK), 1)
        valid = jnp.abs(qpos - kpos) <= SWA_WINDOW
        q = jnp.concatenate([q_ref[rows[t], g * HEAD_DIM:(g + 1) * HEAD_DIM] for g in range(SWA_GROUP)], axis=0)
        s = lax.dot_general(k, q, (((1,), (1,)), ((), ())), preferred_element_type=F32) * scale
        scores.append(jnp.where(jnp.concatenate([valid] * SWA_GROUP, axis=1), s, NEG_INF))
    maxes = [jnp.maximum(jnp.max(s, axis=0, keepdims=True), sink) for s in scores]
    probs = [jnp.exp(s - m) for s, m in zip(scores, maxes)]
    denoms = [jnp.sum(p, axis=0, keepdims=True) + jnp.exp(sink - m) for p, m in zip(probs, maxes)]
    outs = [jnp.dot(vt, (p * (1.0 / d)).astype(BF16), preferred_element_type=F32)
            for p, d, vt in zip(probs, denoms, values)]
    for t in blocks:
        for g in range(SWA_GROUP):
            o_ref[rows[t], g * HEAD_DIM:(g + 1) * HEAD_DIM] = (
                outs[t][:, g * SWA_BLOCK:(g + 1) * SWA_BLOCK].T.astype(o_ref.dtype))


def _swa_attention(qkv, sink, batch, seq, *, qblocks=4):
    T = batch * seq
    nb = seq // (SWA_BLOCK * qblocks)
    gw = SWA_GROUP * HEAD_DIM
    k_col0 = SWA_Q_W // HEAD_DIM
    v_col0 = (SWA_Q_W + SWA_KV_W) // HEAD_DIM
    sink_b = jnp.broadcast_to(sink.astype(F32).reshape(SWA_KV_HEADS, 1, SWA_GROUP, 1),
                              (SWA_KV_HEADS, 1, SWA_GROUP, SWA_BLOCK)).reshape(SWA_KV_HEADS, 1, gw)
    return pl.pallas_call(
        functools.partial(_swa_kernel, seq=seq, qblocks=qblocks),
        grid=(batch, SWA_KV_HEADS, nb),
        in_specs=[
            pl.BlockSpec((SWA_BLOCK * qblocks, gw), lambda b, h, n: (b * nb + n, h)),
            pl.BlockSpec((seq, HEAD_DIM), lambda b, h, n: (b, k_col0 + h)),
            pl.BlockSpec((seq, HEAD_DIM), lambda b, h, n: (b, v_col0 + h)),
            pl.BlockSpec((1, 1, gw), lambda b, h, n: (h, 0, 0)),
        ],
        out_specs=pl.BlockSpec((SWA_BLOCK * qblocks, gw), lambda b, h, n: (b * nb + n, h)),
        out_shape=jax.ShapeDtypeStruct((T, SWA_Q_W), BF16),
        compiler_params=_params(("arbitrary", "arbitrary", "arbitrary")),
        name="swa_attention",
    )(qkv, qkv, qkv, sink_b)


def _nat_tables(seq):
    rows = seq // GRID_W
    kr = min(NAT_KR_MAX, rows)
    span = min(rows, NAT_QROWS + kr - 1)
    nb = rows // NAT_QROWS
    qbl = NAT_QROWS * GRID_W
    r0 = np.arange(nb) * NAT_QROWS
    kstart = np.clip(r0 - kr // 2, 0, rows - span)
    key_rows = kstart[:, None] + np.arange(span)[None, :]
    q_row = r0[:, None] + (np.arange(qbl) // GRID_W)[None, :]
    q_col = np.broadcast_to((np.arange(qbl) % GRID_W)[None, :], q_row.shape)
    k_row = np.repeat(key_rows, GRID_W, axis=1)
    k_col = np.broadcast_to(np.tile(np.arange(GRID_W), span)[None, :], k_row.shape)
    rs = np.clip(q_row - kr // 2, 0, rows - kr)[:, :, None]
    cs = np.clip(q_col - NAT_KC // 2, 0, GRID_W - NAT_KC)[:, :, None]
    kr3, kc3 = k_row[:, None, :], k_col[:, None, :]
    valid = (kr3 >= rs) & (kr3 < rs + kr) & (kc3 >= cs) & (kc3 < cs + NAT_KC)
    dr = np.clip(kr3 - q_row[:, :, None] + NAT_KR_MAX - 1, 0, 2 * NAT_KR_MAX - 2)
    dc = np.clip(kc3 - q_col[:, :, None] + NAT_KC - 1, 0, 2 * NAT_KC - 2)
    flat = np.where(valid, dr * (2 * NAT_KC - 1) + dc, -1).reshape(nb, -1)
    _, first, inverse = np.unique(flat, axis=0, return_index=True, return_inverse=True)
    valid, dr, dc = valid[first], dr[first], dc[first]
    dr_tab = dr[:, ::GRID_W, ::GRID_W]
    assert (dr == np.repeat(np.repeat(dr_tab, GRID_W, axis=1), GRID_W, axis=2)).all()
    col = np.arange(GRID_W)
    toeplitz = np.clip(col[None, :] - col[:, None] + NAT_KC - 1, 0, 2 * NAT_KC - 2)
    assert (dc == np.tile(toeplitz, (NAT_QROWS, span))[None]).all()
    far = np.tile(np.abs(col[None, :] - col[:, None]) >= NAT_KC, (NAT_QROWS, span))
    assert not (valid & far[None]).any()
    return dict(nb=nb, qbl=qbl, span=span, nkeys=span * GRID_W, start=(kstart * GRID_W).astype(np.int32),
                pid=np.asarray(inverse).reshape(-1).astype(np.int32), valid=valid, dr_tab=dr_tab)


def _nat_bias_kernel(rows_ref, valid_ref, o_ref, *, span):
    n_pairs = -(-span // 2)
    for h in range(o_ref.shape[1]):
        for qr in range(NAT_QROWS):
            rs = slice(qr * GRID_W, (qr + 1) * GRID_W)
            for pair in range(n_pairs):
                row = rows_ref[0, h, qr * n_pairs + pair:qr * n_pairs + pair + 1, :]
                blk = pltpu.roll(jnp.broadcast_to(row, (GRID_W, LANES)), LANES - (NAT_KC - 1), 1,
                                 stride=1, stride_axis=0)
                width = min(LANES, span * GRID_W - pair * LANES)
                cs = slice(pair * LANES, pair * LANES + width)
                o_ref[0, h, rs, cs] = jnp.where(valid_ref[0, rs, cs] > 0, blk[:, :width], NEG_INF)


def _nat_bias_table(rel_bias, tab):
    span, qbl, nkeys = tab["span"], tab["qbl"], tab["nkeys"]
    n_pat = tab["valid"].shape[0]
    n_pairs = -(-span // 2)
    nrel = 2 * NAT_KC - 1
    n_rows = -(-NAT_QROWS * n_pairs // 8) * 8
    assert 2 * GRID_W == LANES and nrel <= GRID_W
    rows = rel_bias.astype(F32)[:, tab["dr_tab"]]
    rows = jnp.pad(rows, ((0, 0), (0, 0), (0, 0), (0, 2 * n_pairs - span), (0, GRID_W - nrel)))
    rows = jnp.transpose(rows.reshape(NAT_HEADS, n_pat, NAT_QROWS * n_pairs, LANES), (1, 0, 2, 3))
    rows = jnp.pad(rows, ((0, 0), (0, 0), (0, n_rows - NAT_QROWS * n_pairs), (0, 0)))
    valid = jnp.asarray(tab["valid"].astype(np.float32))
    return pl.pallas_call(
        functools.partial(_nat_bias_kernel, span=span),
        grid=(n_pat,),
        in_specs=[
            pl.BlockSpec((1, NAT_HEADS, n_rows, LANES), lambda p: (p, 0, 0, 0)),
            pl.BlockSpec((1, qbl, nkeys), lambda p: (p, 0, 0)),
        ],
        out_specs=pl.BlockSpec((1, NAT_HEADS, qbl, nkeys), lambda p: (p, 0, 0, 0)),
        out_shape=jax.ShapeDtypeStruct((n_pat, NAT_HEADS, qbl, nkeys), F32),
        compiler_params=_params(("arbitrary",)),
        name="nat_bias_table",
    )(rows, valid)


def _nat_kernel(pid_ref, start_ref, q_ref, k_ref, v_ref, *rest, nkeys, heads, qbl):
    del pid_ref
    bias_refs, o_ref = rest[:-1], rest[-1]
    scale = HEAD_DIM ** -0.5
    sls = [slice(h * HEAD_DIM, (h + 1) * HEAD_DIM) for h in range(heads)]
    blocks = range(len(bias_refs))
    rows = [slice(t * qbl, (t + 1) * qbl) for t in blocks]
    scores, values = [], []
    for t in blocks:
        start = pl.multiple_of(start_ref[pl.program_id(2) * len(bias_refs) + t], GRID_W)
        q = jnp.stack([q_ref[rows[t], sl] for sl in sls], axis=0)
        k = jnp.stack([k_ref[pl.ds(start, nkeys), sl] for sl in sls], axis=0)
        values.append(jnp.stack([v_ref[pl.ds(start, nkeys), sl] for sl in sls], axis=0))
        s = lax.dot_general(q, k, (((2,), (2,)), ((0,), (0,))), preferred_element_type=F32) * scale
        scores.append(s + bias_refs[t][0])
    maxes = [jnp.max(s, axis=-1, keepdims=True) for s in scores]
    probs = [jnp.exp(s - m) for s, m in zip(scores, maxes)]
    probs = [(p * (1.0 / jnp.sum(p, axis=-1, keepdims=True))).astype(BF16) for p in probs]
    outs = [lax.dot_general(p, v, (((2,), (1,)), ((0,), (0,))), preferred_element_type=F32)
            for p, v in zip(probs, values)]
    for t in blocks:
        for h, sl in enumerate(sls):
            o_ref[rows[t], sl] = outs[t][h].astype(o_ref.dtype)


def _nat_attention(qkv, rel_bias, batch, seq, *, qblocks=4):
    T = batch * seq
    tab = _nat_tables(seq)
    qbl, nkeys = tab["qbl"], tab["nkeys"]
    nb = tab["nb"] // qblocks
    bias = _nat_bias_table(rel_bias, tab)
    hpg = 4
    gw = hpg * HEAD_DIM
    n_hg = NAT_HEADS // hpg
    q0 = (SWA_Q_W + 2 * SWA_KV_W) // gw
    k0 = q0 + n_hg
    v0 = k0 + n_hg

    def bias_spec(t):
        return pl.BlockSpec((1, hpg, qbl, nkeys), lambda b, g, n, pid, st: (pid[n * qblocks + t], g, 0, 0))

    grid_spec = pltpu.PrefetchScalarGridSpec(
        num_scalar_prefetch=2,
        grid=(batch, n_hg, nb),
        in_specs=[
            pl.BlockSpec((qbl * qblocks, gw), lambda b, g, n, pid, st: (b * nb + n, q0 + g)),
            pl.BlockSpec((seq, gw), lambda b, g, n, pid, st: (b, k0 + g)),
            pl.BlockSpec((seq, gw), lambda b, g, n, pid, st: (b, v0 + g)),
        ] + [bias_spec(t) for t in range(qblocks)],
        out_specs=pl.BlockSpec((qbl * qblocks, gw), lambda b, g, n, pid, st: (b * nb + n, g)),
    )
    return pl.pallas_call(
        functools.partial(_nat_kernel, nkeys=nkeys, heads=hpg, qbl=qbl),
        grid_spec=grid_spec,
        out_shape=jax.ShapeDtypeStruct((T, NAT_W), BF16),
        compiler_params=_params(("arbitrary", "arbitrary", "arbitrary")),
        name="nat_attention",
    )(jnp.asarray(tab["pid"]), jnp.asarray(tab["start"]), qkv, qkv, qkv, *([bias] * qblocks))


def _mem_block_kernel(h_ref, g_ref, wq_ref, kv_ref, wo_ref, o_ref):
    x = h_ref[...]
    ms = jnp.mean(x * x, axis=-1, keepdims=True)
    xn = ((x * lax.rsqrt(ms + RMS_EPS)) * g_ref[...]).astype(BF16)
    q = jnp.dot(xn, wq_ref[...], preferred_element_type=F32).astype(BF16)
    scale = MEM_HEAD_DIM ** -0.5
    heads = []
    for h in range(MEM_HEADS):
        sl = slice(h * MEM_HEAD_DIM, (h + 1) * MEM_HEAD_DIM)
        k = kv_ref[:, sl]
        v = kv_ref[:, MEM_W + h * MEM_HEAD_DIM:MEM_W + (h + 1) * MEM_HEAD_DIM]
        s = lax.dot_general(q[:, sl], k, (((1,), (1,)), ((), ())), preferred_element_type=F32) * scale
        m = jnp.max(s, axis=-1, keepdims=True)
        p = jnp.exp(s - m)
        p = (p * (1.0 / jnp.sum(p, axis=-1, keepdims=True))).astype(BF16)
        heads.append(jnp.dot(p, v, preferred_element_type=F32).astype(BF16))
    om = jnp.concatenate(heads, axis=-1)
    o_ref[...] = x + jnp.dot(om, wo_ref[...], preferred_element_type=F32)


def _mem_block(h, gain, w_q, kv, w_o, batch, seq, mem_len, *, bq):
    T, D = h.shape
    nq = seq // bq
    once = pl.Buffered(1)
    row_spec = pl.BlockSpec((bq, D), lambda b, i: (b * nq + i, 0))
    return pl.pallas_call(
        _mem_block_kernel,
        grid=(batch, nq),
        in_specs=[
            row_spec,
            pl.BlockSpec((1, D), lambda b, i: (0, 0)),
            pl.BlockSpec((D, MEM_W), lambda b, i: (0, 0), pipeline_mode=once),
            pl.BlockSpec((mem_len, 2 * MEM_W), lambda b, i: (b, 0)),
            pl.BlockSpec((MEM_W, D), lambda b, i: (0, 0), pipeline_mode=once),
        ],
        out_specs=row_spec,
        out_shape=jax.ShapeDtypeStruct((T, D), F32),
        compiler_params=_params(("arbitrary", "arbitrary")),
        name="mem_block",
    )(h, gain.reshape(1, D).astype(F32), w_q, kv, w_o)


def _extract_max(work, rows):
    m = jnp.max(work, axis=0, keepdims=True)
    first = jnp.min(jnp.where(work == m, rows, float(work.shape[0])), axis=0, keepdims=True)
    return m, rows == first


def _top16_rows(scores, dests, work_scr, rank_scr, base):
    shape = scores[0].shape
    rows = lax.broadcasted_iota(jnp.int32, shape, 0).astype(F32)
    chains = range(len(scores))

    def run(first_only):
        for n, s in enumerate(scores):
            work_scr[base + n] = s
            rank_scr[base + n] = jnp.full(shape, float(PEER_TOPK), F32)

        def body(a, carry):
            for n in chains:
                work = work_scr[base + n]
                if first_only:
                    m, hit = _extract_max(work, rows)
                else:
                    m = jnp.max(work, axis=0, keepdims=True)
                    hit = work == m
                ref, idx = dests[n]
                ref[idx, pl.ds(a, 1), :] = m
                work_scr[base + n] = jnp.where(hit, -jnp.inf, work)
                rank_scr[base + n] = jnp.where(hit, jnp.asarray(a, F32), rank_scr[base + n])
            return carry

        lax.fori_loop(0, PEER_TOPK, body, 0)

    run(first_only=False)
    excess = jnp.zeros((1, shape[1]), F32)
    for n in chains:
        taken = jnp.sum(jnp.where(rank_scr[base + n] < float(PEER_TOPK), 1.0, 0.0), axis=0, keepdims=True)
        excess = jnp.maximum(excess, jnp.abs(taken - float(PEER_TOPK)))
    pl.when(jnp.max(excess) > 0.0)(lambda: run(first_only=True))
    return [rank_scr[base + n] for n in chains]


def _route_kernel(q_ref, keys_ref, n1_ref, e1_ref, r2_ref, e2_ref, v1_scr, v2_scr, c_scr, sel_scr, work_scr, rank_scr,
                  *, tb, group):
    half = PEER_QDIM // 2
    dn = (((1,), (1,)), ((), ()))
    s1_all = lax.dot_general(keys_ref[0, 0], q_ref[:, :half], dn, preferred_element_type=F32)
    s2_all = lax.dot_general(keys_ref[0, 1], q_ref[:, half:], dn, preferred_element_type=F32)
    n_c = tb // LANES
    cols = [slice(c * LANES, (c + 1) * LANES) for c in range(n_c)]
    s1 = [s1_all[:, cl] for cl in cols]
    s2 = [s2_all[:, cl] for cl in cols]
    rank1, rank2 = [], []
    for c0 in range(0, n_c, group):
        cs = list(range(c0, c0 + group))
        ranks = _top16_rows([s1[c] for c in cs] + [s2[c] for c in cs],
                            [(v1_scr, c) for c in cs] + [(v2_scr, c) for c in cs], work_scr, rank_scr, 2 * c0)
        rank1 += ranks[:group]
        rank2 += ranks[group:]
    for c in range(n_c):
        for p, (a, b) in enumerate(_CELLS):
            c_scr[c, p:p + 1, :] = v1_scr[c, a:a + 1, :] + v2_scr[c, b:b + 1, :]
        if _CELL_ROWS > _N_CELLS:
            c_scr[c, _N_CELLS:, :] = jnp.full((_CELL_ROWS - _N_CELLS, LANES), -jnp.inf, F32)
    cand = [c_scr[c] for c in range(n_c)]
    prow = lax.broadcasted_iota(jnp.int32, cand[0].shape, 0).astype(F32)

    def body(t, carry):
        out = []
        for work, picked in carry:
            _, hit = _extract_max(work, prow)
            out.append((jnp.where(hit, -jnp.inf, work), jnp.where(hit, 1.0, picked)))
        return tuple(out)

    picked = [pk for _, pk in lax.fori_loop(0, PEER_TOPK, body,
                                            tuple((cd, jnp.zeros(cd.shape, F32)) for cd in cand))]
    for c in range(n_c):
        top1 = v1_scr[c, 0:1, :]
        top2 = v2_scr[c, 0:1, :]
        z = jnp.sum(jnp.where(picked[c] > 0, jnp.exp(cand[c] - (top1 + top2)), 0.0), axis=0, keepdims=True)
        sel_scr[c] = picked[c]
        n1 = jnp.zeros(s1[c].shape, F32)
        p0 = 0
        for a in range(PEER_TOPK):
            width = sum(1 for (aa, _) in _CELLS if aa == a)
            n_a = jnp.sum(sel_scr[c, p0:p0 + width, :], axis=0, keepdims=True)
            n1 = jnp.where(rank1[c] == float(a), n_a, n1)
            p0 += width
        n1_ref[0, :, cols[c]] = n1
        e1_ref[0, :, cols[c]] = jnp.exp(s1[c] - top1) / z
        r2_ref[0, :, cols[c]] = rank2[c].astype(r2_ref.dtype)
        e2_ref[0, :, cols[c]] = jnp.exp(s2[c] - top2).astype(e2_ref.dtype)


def _peer_route(q, sub_keys, *, tb=512, group=4):
    T = q.shape[0]
    shape = jax.ShapeDtypeStruct((PEER_HEADS, PEER_NKEYS, T), F32)
    shape_packed = jax.ShapeDtypeStruct((PEER_HEADS, PEER_NKEYS, T), BF16)
    out_spec = pl.BlockSpec((1, PEER_NKEYS, tb), lambda i, h: (h, 0, i))
    return pl.pallas_call(
        functools.partial(_route_kernel, tb=tb, group=group),
        grid=(T // tb, PEER_HEADS),
        in_specs=[
            pl.BlockSpec((tb, PEER_QDIM), lambda i, h: (i, h)),
            pl.BlockSpec((1, 2, PEER_NKEYS, PEER_QDIM // 2), lambda i, h: (h, 0, 0, 0)),
        ],
        out_specs=[out_spec] * 4,
        out_shape=[shape, shape, shape_packed, shape_packed],
        scratch_shapes=[pltpu.VMEM((tb // LANES, PEER_TOPK, LANES), F32),
                        pltpu.VMEM((tb // LANES, PEER_TOPK, LANES), F32),
                        pltpu.VMEM((tb // LANES, _CELL_ROWS, LANES), F32),
                        pltpu.VMEM((tb // LANES, _CELL_ROWS, LANES), F32),
                        pltpu.VMEM((2 * (tb // LANES), PEER_NKEYS, LANES), F32),
                        pltpu.VMEM((2 * (tb // LANES), PEER_NKEYS, LANES), F32)],
        compiler_params=_params(("arbitrary", "arbitrary")),
        name="peer_route",
    )(q, sub_keys)


def _peer_kernel(xnt_ref, u_ref, vt_ref, n1_ref, e1_ref, r2_ref, e2_ref, o_ref, a_scr, w_scr, *, eb, tc):
    k = pl.program_id(1)

    @pl.when(k == 0)
    def _():
        o_ref[...] = jnp.zeros(o_ref.shape, F32)
        a_scr[...] = jnp.dot(u_ref[...], xnt_ref[...], preferred_element_type=F32)

    @pl.when(k > 0)
    def _():
        n_i = eb // PEER_NKEYS
        pack = 16
        groups = PEER_NKEYS // pack
        for c in range(a_scr.shape[1] // tc):
            cols = slice(c * tc, (c + 1) * tc)
            for a in range(n_i):
                rows = slice(a * PEER_NKEYS, (a + 1) * PEER_NKEYS)
                i_row = k * n_i + (a - n_i)
                a_t = a_scr[rows, cols]
                act = 0.5 * a_t * (1.0 + lax.erf(a_t * np.float32(np.sqrt(0.5))))
                gate = jnp.zeros((PEER_NKEYS, tc), BF16)
                for h in range(PEER_HEADS):
                    cnt = jnp.broadcast_to(n1_ref[h, pl.ds(i_row, 1), :][:, cols], (pack, tc)).astype(BF16)
                    g1 = jnp.broadcast_to(e1_ref[h, pl.ds(i_row, 1), :][:, cols], (pack, tc)).astype(BF16)
                    cnt = jnp.concatenate([cnt] * groups, axis=0)
                    g1 = jnp.concatenate([g1] * groups, axis=0)
                    gate = gate + jnp.where(r2_ref[h, :, cols] < cnt, e2_ref[h, :, cols] * g1,
                                            jnp.zeros((), BF16))
                w_scr[rows, cols] = (act * gate.astype(F32)).astype(BF16)
        a_scr[...] = jnp.dot(u_ref[...], xnt_ref[...], preferred_element_type=F32)
        o_ref[...] += jnp.dot(vt_ref[0], w_scr[...], preferred_element_type=F32)


def _peer_mix(xnt, u, vt, n1, e1, r2, e2, *, tb=512, tc=512):
    D, T = xnt.shape
    E = u.shape[0]
    n_e, _, eb = vt.shape
    once = pl.Buffered(1)
    route_spec = pl.BlockSpec((PEER_HEADS, PEER_NKEYS, tb), lambda i, k: (0, 0, i), pipeline_mode=once)
    return pl.pallas_call(
        functools.partial(_peer_kernel, eb=eb, tc=tc),
        grid=(T // tb, n_e + 1),
        in_specs=[
            pl.BlockSpec((D, tb), lambda i, k: (0, i), pipeline_mode=once),
            pl.BlockSpec((eb, D), lambda i, k: (jnp.minimum(k, n_e - 1), 0)),
            pl.BlockSpec((1, D, eb), lambda i, k: (jnp.maximum(k - 1, 0), 0, 0)),
            route_spec, route_spec, route_spec, route_spec,
        ],
        out_specs=pl.BlockSpec((D, tb), lambda i, k: (0, i)),
        out_shape=jax.ShapeDtypeStruct((D, T), F32),
        scratch_shapes=[pltpu.VMEM((eb, tb), F32), pltpu.VMEM((eb, tb), BF16)],
        compiler_params=_params(("arbitrary", "arbitrary")),
        name="peer_mix",
    )(xnt, u, vt, n1, e1, r2, e2)


def _transpose_cast_kernel(x_ref, o_ref):
    o_ref[0] = x_ref[...].T.astype(o_ref.dtype)


def _transpose_cast_tiles(x, rows):
    N, D = x.shape
    return pl.pallas_call(
        _transpose_cast_kernel,
        grid=(N // rows,),
        in_specs=[pl.BlockSpec((rows, D), lambda e: (e, 0))],
        out_specs=pl.BlockSpec((1, D, rows), lambda e: (e, 0, 0)),
        out_shape=jax.ShapeDtypeStruct((N // rows, D, rows), BF16),
        compiler_params=_params(("arbitrary",)),
        name="transpose_cast",
    )(x)


def _add_norm_kernel(a_ref, bt_ref, g_ref, o_ref, *, norm):
    x = a_ref[...] + bt_ref[...].T
    if norm:
        ms = jnp.mean(x * x, axis=-1, keepdims=True)
        x = (x * lax.rsqrt(ms + RMS_EPS)) * g_ref[...]
    o_ref[...] = x


def _add_norm(a, bt, gain, *, norm, bm=256):
    T, D = a.shape
    spec = pl.BlockSpec((bm, D), lambda i: (i, 0))
    return pl.pallas_call(
        functools.partial(_add_norm_kernel, norm=norm),
        grid=(T // bm,),
        in_specs=[spec, pl.BlockSpec((D, bm), lambda i: (0, i)), pl.BlockSpec((1, D), lambda i: (0, 0))],
        out_specs=spec,
        out_shape=jax.ShapeDtypeStruct((T, D), F32),
        compiler_params=_params(("arbitrary",)),
        name="add_norm",
    )(a, bt, gain.reshape(1, D).astype(F32))


def _rope_tables(seq):
    half = HEAD_DIM // 2
    inv = ROPE_THETA ** (-jnp.arange(half, dtype=F32) / half)
    ang = jnp.arange(seq, dtype=jnp.int32).astype(F32)[:, None] * inv[None, :]
    cos, sin = jnp.cos(ang), jnp.sin(ang)
    return jnp.concatenate([cos, cos], axis=-1), jnp.concatenate([-sin, sin], axis=-1)


def kernel(x, mem, norm_mix, w_in, swa_sink, nat_rel_bias, out_norm_swa, out_norm_nat, w_out, norm_mem_q,
           norm_mem_kv, w_mem_q, w_mem_k, w_mem_v, w_mem_o, norm_ffn, w_peer_q, peer_sub_keys, peer_u, peer_v,
           norm_final):
    B, S, D = x.shape
    M = mem.shape[1]
    T = B * S
    depth = w_in.shape[0]
    assert depth >= 1
    rope = _rope_tables(S)
    h = x.reshape(T, D)
    mem2 = mem.reshape(B * M, D)
    rope_cols = SWA_Q_W + SWA_KV_W
    for l in range(depth):
        qkv = _rms_matmul([h], [norm_mix[l]], w_in[l].astype(BF16), **TILES["in_proj"], out_dtype=BF16,
                          rope=rope, rope_cols=rope_cols, name="in_proj")
        oa = _swa_attention(qkv, swa_sink[l], B, S)
        ob = _nat_attention(qkv, nat_rel_bias[l], B, S)
        h = _rms_matmul([oa, ob], [out_norm_swa[l], out_norm_nat[l]], w_out[l].astype(BF16), **TILES["out_proj"],
                        out_dtype=F32, res=h, name="out_proj")
        w_kv = jnp.concatenate([w_mem_k[l], w_mem_v[l]], axis=1).astype(BF16)
        kv = _rms_matmul([mem2], [norm_mem_kv[l]], w_kv, **TILES["mem_kv_proj"], out_dtype=BF16,
                         name="mem_kv_proj")
        h = _mem_block(h, norm_mem_q[l], w_mem_q[l].astype(BF16), kv, w_mem_o[l].astype(BF16), B, S, M,
                       bq=MEM_QUERY_BLOCK)
        qp, xn = _rms_matmul([h], [norm_ffn[l]], w_peer_q[l].astype(BF16), **TILES["peer_q_proj"],
                             out_dtype=BF16, emit_xn=True, name="peer_q_proj")
        n1, e1, r2, e2 = _peer_route(qp, peer_sub_keys[l].astype(BF16))
        vt = _transpose_cast_tiles(peer_v[l], PEER_EXPERT_TILE)
        peer = _peer_mix(xn.T, peer_u[l].astype(BF16), vt, n1, e1, r2, e2)
        last = l == depth - 1
        h = _add_norm(h, peer, norm_final if last else jnp.ones((D,), F32), norm=last)
    return h.reshape(B, S, D)
```

```python
import functools

import numpy as np
import jax
import jax.numpy as jnp
from jax import lax
from jax.experimental import pallas as pl
from jax.experimental.pallas import tpu as pltpu

F32 = jnp.float32
BF16 = jnp.bfloat16

D_MODEL = 4096
HEAD_DIM = 128
SWA_Q_HEADS = 16
SWA_KV_HEADS = 4
SWA_GROUP = SWA_Q_HEADS // SWA_KV_HEADS
SWA_WINDOW = 128
SWA_BLOCK = 128
ROPE_THETA = 10000.0
NAT_HEADS = 16
NAT_KR_MAX = 8
NAT_KC = 16
GRID_W = 64
NAT_QROWS = 2
SWA_Q_W = SWA_Q_HEADS * HEAD_DIM
SWA_KV_W = SWA_KV_HEADS * HEAD_DIM
NAT_W = NAT_HEADS * HEAD_DIM
MEM_HEADS = 4
MEM_HEAD_DIM = 256
MEM_W = MEM_HEADS * MEM_HEAD_DIM
PEER_HEADS = 8
PEER_NKEYS = 128
PEER_QDIM = 256
PEER_TOPK = 16
RMS_EPS = 1e-6
NEG_INF = -1e30

LANES = 128
VMEM_LIMIT_BYTES = 56 * 1024 * 1024

TILES = {
    "in_proj": dict(bm=512, bn=1024),
    "out_proj": dict(bm=1024, bn=512),
    "mem_kv_proj": dict(bm=512, bn=512),
    "peer_q_proj": dict(bm=512, bn=PEER_HEADS * PEER_QDIM),
}
MEM_QUERY_BLOCK = 512
PEER_EXPERT_TILE = 512

_CELLS = [(a, b) for a in range(PEER_TOPK) for b in range(PEER_TOPK) if (a + 1) * (b + 1) <= PEER_TOPK]
_N_CELLS = len(_CELLS)
_CELL_ROWS = -(-_N_CELLS // 8) * 8


def _params(sem, vmem_limit_bytes=None):
    return pltpu.CompilerParams(dimension_semantics=sem, vmem_limit_bytes=vmem_limit_bytes or VMEM_LIMIT_BYTES)


def _rms_matmul_kernel(*refs, seg_widths, norm, has_res, rope_cols, emit_xn):
    n_seg = len(seg_widths)
    pos = 0
    x_refs = refs[pos:pos + n_seg]; pos += n_seg
    g_refs = ()
    if norm:
        g_refs = refs[pos:pos + n_seg]; pos += n_seg
    w_ref = refs[pos]; pos += 1
    res_ref = None
    if has_res:
        res_ref = refs[pos]; pos += 1
    cos_ref = sin_ref = None
    if rope_cols:
        cos_ref, sin_ref = refs[pos], refs[pos + 1]; pos += 2
    o_ref = refs[pos]; pos += 1
    xn_out_ref = None
    if emit_xn:
        xn_out_ref = refs[pos]; pos += 1
    xn_ref = refs[pos]

    j = pl.program_id(1)

    @pl.when(j == 0)
    def _():
        off = 0
        for s in range(n_seg):
            x = x_refs[s][...].astype(F32)
            if norm:
                ms = jnp.mean(x * x, axis=-1, keepdims=True)
                x = (x * lax.rsqrt(ms + RMS_EPS)) * g_refs[s][...]
            xn_ref[:, off:off + seg_widths[s]] = x.astype(BF16)
            off += seg_widths[s]
        if emit_xn:
            xn_out_ref[...] = xn_ref[...]

    acc = jnp.dot(xn_ref[...], w_ref[...], preferred_element_type=F32)
    if has_res:
        acc = acc + res_ref[...]

    bn = acc.shape[1]
    if rope_cols:
        def store(n_roped_heads):
            for g in range(n_roped_heads):
                sl = slice(g * HEAD_DIM, (g + 1) * HEAD_DIM)
                rot = pltpu.roll(acc[:, sl], HEAD_DIM // 2, 1)
                o_ref[:, sl] = (acc[:, sl] * cos_ref[...] + rot * sin_ref[...]).astype(o_ref.dtype)
            if n_roped_heads * HEAD_DIM < bn:
                o_ref[:, n_roped_heads * HEAD_DIM:] = acc[:, n_roped_heads * HEAD_DIM:].astype(o_ref.dtype)

        full_blocks, rem = divmod(rope_cols, bn)
        pl.when(j < full_blocks)(lambda: store(bn // HEAD_DIM))
        if rem:
            pl.when(j == full_blocks)(lambda: store(rem // HEAD_DIM))
        pl.when(j >= full_blocks + (1 if rem else 0))(lambda: store(0))
    else:
        o_ref[...] = acc.astype(o_ref.dtype)


def _rms_matmul(xs, gains, w, *, bm, bn, out_dtype, res=None, rope=None, rope_cols=0, emit_xn=False, name):
    T = xs[0].shape[0]
    seg_widths = tuple(int(x.shape[1]) for x in xs)
    K = sum(seg_widths)
    N = w.shape[1]
    norm = gains is not None
    assert w.shape[0] == K and T % bm == 0 and N % bn == 0 and rope_cols % HEAD_DIM == 0
    in_specs = [pl.BlockSpec((bm, kw), lambda i, j: (i, 0)) for kw in seg_widths]
    args = list(xs)
    if norm:
        in_specs += [pl.BlockSpec((1, kw), lambda i, j: (0, 0)) for kw in seg_widths]
        args += [g.reshape(1, -1).astype(F32) for g in gains]
    in_specs.append(pl.BlockSpec((K, bn), lambda i, j: (0, j), pipeline_mode=pl.Buffered(1) if bn == N else None))
    args.append(w)
    if res is not None:
        in_specs.append(pl.BlockSpec((bm, bn), lambda i, j: (i, j)))
        args.append(res)
    if rope_cols:
        cos, sin = rope
        pos_blocks = cos.shape[0] // bm
        in_specs += [pl.BlockSpec((bm, HEAD_DIM), lambda i, j: (i % pos_blocks, 0))] * 2
        args += [cos, sin]
    out_shape = [jax.ShapeDtypeStruct((T, N), out_dtype)]
    out_specs = [pl.BlockSpec((bm, bn), lambda i, j: (i, j))]
    if emit_xn:
        out_shape.append(jax.ShapeDtypeStruct((T, K), BF16))
        out_specs.append(pl.BlockSpec((bm, K), lambda i, j: (i, 0)))
    kern = functools.partial(_rms_matmul_kernel, seg_widths=seg_widths, norm=norm, has_res=res is not None,
                             rope_cols=rope_cols, emit_xn=emit_xn)
    outs = pl.pallas_call(
        kern,
        grid=(T // bm, N // bn),
        in_specs=in_specs,
        out_specs=out_specs,
        out_shape=out_shape,
        scratch_shapes=[pltpu.VMEM((bm, K), BF16)],
        compiler_params=_params(("arbitrary", "arbitrary")),
        name=name,
    )(*args)
    return outs if emit_xn else outs[0]


def _swa_kernel(q_ref, k_ref, v_ref, sink_ref, o_ref, *, seq, qblocks):
    band = 3 * SWA_BLOCK
    scale = HEAD_DIM ** -0.5
    sink = sink_ref[0]
    blocks = range(qblocks)
    rows = [slice(t * SWA_BLOCK, (t + 1) * SWA_BLOCK) for t in blocks]
    scores, values = [], []
    for t in blocks:
        n = pl.program_id(2) * qblocks + t
        start = pl.multiple_of(jnp.clip((n - 1) * SWA_BLOCK, 0, seq - band), SWA_BLOCK)
        k = k_ref[pl.ds(start, band), :]
        values.append(v_ref[pl.ds(start, band), :].astype(F32).T.astype(BF16))
        kpos = start + lax.broadcasted_iota(jnp.int32, (band, SWA_BLOCK), 0)
        qpos = n * SWA_BLOCK + lax.broadcasted_iota(jnp.int32, (band, SWA_BLOCK), 1)
        valid = jnp.abs(qpos - kpos) <= SWA_WINDOW
        q = jnp.concatenate([q_ref[rows[t], g * HEAD_DIM:(g + 1) * HEAD_DIM] for g in range(SWA_GROUP)], axis=0)
        s = lax.dot_general(k, q, (((1,), (1,)), ((), ())), preferred_element_type=F32) * scale
        scores.append(jnp.where(jnp.concatenate([valid] * SWA_GROUP, axis=1), s, NEG_INF))
    maxes = [jnp.maximum(jnp.max(s, axis=0, keepdims=True), sink) for s in scores]
    probs = [jnp.exp(s - m) for s, m in zip(scores, maxes)]
    denoms = [jnp.sum(p, axis=0, keepdims=True) + jnp.exp(sink - m) for p, m in zip(probs, maxes)]
    outs = [jnp.dot(vt, (p * (1.0 / d)).astype(BF16), preferred_element_type=F32)
            for p, d, vt in zip(probs, denoms, values)]
    for t in blocks:
        for g in range(SWA_GROUP):
            o_ref[rows[t], g * HEAD_DIM:(g + 1) * HEAD_DIM] = (
                outs[t][:, g * SWA_BLOCK:(g + 1) * SWA_BLOCK].T.astype(o_ref.dtype))


def _swa_attention(qkv, sink, batch, seq, *, qblocks=8):
    T = batch * seq
    nb = seq // (SWA_BLOCK * qblocks)
    gw = SWA_GROUP * HEAD_DIM
    k_col0 = SWA_Q_W // HEAD_DIM
    v_col0 = (SWA_Q_W + SWA_KV_W) // HEAD_DIM
    sink_b = jnp.broadcast_to(sink.astype(F32).reshape(SWA_KV_HEADS, 1, SWA_GROUP, 1),
                              (SWA_KV_HEADS, 1, SWA_GROUP, SWA_BLOCK)).reshape(SWA_KV_HEADS, 1, gw)
    return pl.pallas_call(
        functools.partial(_swa_kernel, seq=seq, qblocks=qblocks),
        grid=(batch, SWA_KV_HEADS, nb),
        in_specs=[
            pl.BlockSpec((SWA_BLOCK * qblocks, gw), lambda b, h, n: (b * nb + n, h)),
            pl.BlockSpec((seq, HEAD_DIM), lambda b, h, n: (b, k_col0 + h)),
            pl.BlockSpec((seq, HEAD_DIM), lambda b, h, n: (b, v_col0 + h)),
            pl.BlockSpec((1, 1, gw), lambda b, h, n: (h, 0, 0)),
        ],
        out_specs=pl.BlockSpec((SWA_BLOCK * qblocks, gw), lambda b, h, n: (b * nb + n, h)),
        out_shape=jax.ShapeDtypeStruct((T, SWA_Q_W), BF16),
        compiler_params=_params(("arbitrary", "arbitrary", "arbitrary")),
        name="swa_attention",
    )(qkv, qkv, qkv, sink_b)


def _nat_tables(seq):
    rows = seq // GRID_W
    kr = min(NAT_KR_MAX, rows)
    span = min(rows, NAT_QROWS + kr - 1)
    nb = rows // NAT_QROWS
    qbl = NAT_QROWS * GRID_W
    r0 = np.arange(nb) * NAT_QROWS
    kstart = np.clip(r0 - kr // 2, 0, rows - span)
    key_rows = kstart[:, None] + np.arange(span)[None, :]
    q_row = r0[:, None] + (np.arange(qbl) // GRID_W)[None, :]
    q_col = np.broadcast_to((np.arange(qbl) % GRID_W)[None, :], q_row.shape)
    k_row = np.repeat(key_rows, GRID_W, axis=1)
    k_col = np.broadcast_to(np.tile(np.arange(GRID_W), span)[None, :], k_row.shape)
    rs = np.clip(q_row - kr // 2, 0, rows - kr)[:, :, None]
    cs = np.clip(q_col - NAT_KC // 2, 0, GRID_W - NAT_KC)[:, :, None]
    kr3, kc3 = k_row[:, None, :], k_col[:, None, :]
    valid = (kr3 >= rs) & (kr3 < rs + kr) & (kc3 >= cs) & (kc3 < cs + NAT_KC)
    dr = np.clip(kr3 - q_row[:, :, None] + NAT_KR_MAX - 1, 0, 2 * NAT_KR_MAX - 2)
    dc = np.clip(kc3 - q_col[:, :, None] + NAT_KC - 1, 0, 2 * NAT_KC - 2)
    flat = np.where(valid, dr * (2 * NAT_KC - 1) + dc, -1).reshape(nb, -1)
    _, first, inverse = np.unique(flat, axis=0, return_index=True, return_inverse=True)
    valid, dr, dc = valid[first], dr[first], dc[first]
    dr_tab = dr[:, ::GRID_W, ::GRID_W]
    assert (dr == np.repeat(np.repeat(dr_tab, GRID_W, axis=1), GRID_W, axis=2)).all()
    col = np.arange(GRID_W)
    toeplitz = np.clip(col[None, :] - col[:, None] + NAT_KC - 1, 0, 2 * NAT_KC - 2)
    assert (dc == np.tile(toeplitz, (NAT_QROWS, span))[None]).all()
    far = np.tile(np.abs(col[None, :] - col[:, None]) >= NAT_KC, (NAT_QROWS, span))
    assert not (valid & far[None]).any()
    return dict(nb=nb, qbl=qbl, span=span, nkeys=span * GRID_W, start=(kstart * GRID_W).astype(np.int32),
                pid=np.asarray(inverse).reshape(-1).astype(np.int32), valid=valid, dr_tab=dr_tab)


def _nat_bias_kernel(rows_ref, valid_ref, o_ref, *, span):
    n_pairs = -(-span // 2)
    for h in range(o_ref.shape[1]):
        for qr in range(NAT_QROWS):
            rs = slice(qr * GRID_W, (qr + 1) * GRID_W)
            for pair in range(n_pairs):
                row = rows_ref[0, h, qr * n_pairs + pair:qr * n_pairs + pair + 1, :]
                blk = pltpu.roll(jnp.broadcast_to(row, (GRID_W, LANES)), LANES - (NAT_KC - 1), 1,
                                 stride=1, stride_axis=0)
                width = min(LANES, span * GRID_W - pair * LANES)
                cs = slice(pair * LANES, pair * LANES + width)
                o_ref[0, h, rs, cs] = jnp.where(valid_ref[0, rs, cs] > 0, blk[:, :width], NEG_INF)


def _nat_bias_table(rel_bias, tab):
    span, qbl, nkeys = tab["span"], tab["qbl"], tab["nkeys"]
    n_pat = tab["valid"].shape[0]
    n_pairs = -(-span // 2)
    nrel = 2 * NAT_KC - 1
    n_rows = -(-NAT_QROWS * n_pairs // 8) * 8
    assert 2 * GRID_W == LANES and nrel <= GRID_W
    rows = rel_bias.astype(F32)[:, tab["dr_tab"]]
    rows = jnp.pad(rows, ((0, 0), (0, 0), (0, 0), (0, 2 * n_pairs - span), (0, GRID_W - nrel)))
    rows = jnp.transpose(rows.reshape(NAT_HEADS, n_pat, NAT_QROWS * n_pairs, LANES), (1, 0, 2, 3))
    rows = jnp.pad(rows, ((0, 0), (0, 0), (0, n_rows - NAT_QROWS * n_pairs), (0, 0)))
    valid = jnp.asarray(tab["valid"].astype(np.float32))
    return pl.pallas_call(
        functools.partial(_nat_bias_kernel, span=span),
        grid=(n_pat,),
        in_specs=[
            pl.BlockSpec((1, NAT_HEADS, n_rows, LANES), lambda p: (p, 0, 0, 0)),
            pl.BlockSpec((1, qbl, nkeys), lambda p: (p, 0, 0)),
        ],
        out_specs=pl.BlockSpec((1, NAT_HEADS, qbl, nkeys), lambda p: (p, 0, 0, 0)),
        out_shape=jax.ShapeDtypeStruct((n_pat, NAT_HEADS, qbl, nkeys), F32),
        compiler_params=_params(("arbitrary",)),
        name="nat_bias_table",
    )(rows, valid)


def _nat_kernel(pid_ref, start_ref, q_ref, k_ref, v_ref, *rest, nkeys, heads, qbl):
    del pid_ref
    bias_refs, o_ref = rest[:-1], rest[-1]
    scale = HEAD_DIM ** -0.5
    sls = [slice(h * HEAD_DIM, (h + 1) * HEAD_DIM) for h in range(heads)]
    blocks = range(len(bias_refs))
    rows = [slice(t * qbl, (t + 1) * qbl) for t in blocks]
    scores, values = [], []
    for t in blocks:
        start = pl.multiple_of(start_ref[pl.program_id(2) * len(bias_refs) + t], GRID_W)
        q = jnp.stack([q_ref[rows[t], sl] for sl in sls], axis=0)
        k = jnp.stack([k_ref[pl.ds(start, nkeys), sl] for sl in sls], axis=0)
        values.append(jnp.stack([v_ref[pl.ds(start, nkeys), sl] for sl in sls], axis=0))
        s = lax.dot_general(q, k, (((2,), (2,)), ((0,), (0,))), preferred_element_type=F32) * scale
        scores.append(s + bias_refs[t][0])
    maxes = [jnp.max(s, axis=-1, keepdims=True) for s in scores]
    probs = [jnp.exp(s - m) for s, m in zip(scores, maxes)]
    probs = [(p * (1.0 / jnp.sum(p, axis=-1, keepdims=True))).astype(BF16) for p in probs]
    outs = [lax.dot_general(p, v, (((2,), (1,)), ((0,), (0,))), preferred_element_type=F32)
            for p, v in zip(probs, values)]
    for t in blocks:
        for h, sl in enumerate(sls):
            o_ref[rows[t], sl] = outs[t][h].astype(o_ref.dtype)


def _nat_attention(qkv, rel_bias, batch, seq, *, qblocks=4):
    T = batch * seq
    tab = _nat_tables(seq)
    qbl, nkeys = tab["qbl"], tab["nkeys"]
    nb = tab["nb"] // qblocks
    bias = _nat_bias_table(rel_bias, tab)
    hpg = 4
    gw = hpg * HEAD_DIM
    n_hg = NAT_HEADS // hpg
    q0 = (SWA_Q_W + 2 * SWA_KV_W) // gw
    k0 = q0 + n_hg
    v0 = k0 + n_hg

    def bias_spec(t):
        return pl.BlockSpec((1, hpg, qbl, nkeys), lambda b, g, n, pid, st: (pid[n * qblocks + t], g, 0, 0))

    grid_spec = pltpu.PrefetchScalarGridSpec(
        num_scalar_prefetch=2,
        grid=(batch, n_hg, nb),
        in_specs=[
            pl.BlockSpec((qbl * qblocks, gw), lambda b, g, n, pid, st: (b * nb + n, q0 + g)),
            pl.BlockSpec((seq, gw), lambda b, g, n, pid, st: (b, k0 + g)),
            pl.BlockSpec((seq, gw), lambda b, g, n, pid, st: (b, v0 + g)),
        ] + [bias_spec(t) for t in range(qblocks)],
        out_specs=pl.BlockSpec((qbl * qblocks, gw), lambda b, g, n, pid, st: (b * nb + n, g)),
    )
    return pl.pallas_call(
        functools.partial(_nat_kernel, nkeys=nkeys, heads=hpg, qbl=qbl),
        grid_spec=grid_spec,
        out_shape=jax.ShapeDtypeStruct((T, NAT_W), BF16),
        compiler_params=_params(("arbitrary", "arbitrary", "arbitrary")),
        name="nat_attention",
    )(jnp.asarray(tab["pid"]), jnp.asarray(tab["start"]), qkv, qkv, qkv, *([bias] * qblocks))


def _mem_block_kernel(h_ref, g_ref, wq_ref, kv_ref, wo_ref, o_ref):
    x = h_ref[...]
    ms = jnp.mean(x * x, axis=-1, keepdims=True)
    xn = ((x * lax.rsqrt(ms + RMS_EPS)) * g_ref[...]).astype(BF16)
    q = jnp.dot(xn, wq_ref[...], preferred_element_type=F32).astype(BF16)
    scale = MEM_HEAD_DIM ** -0.5
    heads = []
    for h in range(MEM_HEADS):
        sl = slice(h * MEM_HEAD_DIM, (h + 1) * MEM_HEAD_DIM)
        k = kv_ref[:, sl]
        v = kv_ref[:, MEM_W + h * MEM_HEAD_DIM:MEM_W + (h + 1) * MEM_HEAD_DIM]
        s = lax.dot_general(q[:, sl], k, (((1,), (1,)), ((), ())), preferred_element_type=F32) * scale
        m = jnp.max(s, axis=-1, keepdims=True)
        p = jnp.exp(s - m)
        p = (p * (1.0 / jnp.sum(p, axis=-1, keepdims=True))).astype(BF16)
        heads.append(jnp.dot(p, v, preferred_element_type=F32).astype(BF16))
    om = jnp.concatenate(heads, axis=-1)
    o_ref[...] = x + jnp.dot(om, wo_ref[...], preferred_element_type=F32)


def _mem_block(h, gain, w_q, kv, w_o, batch, seq, mem_len, *, bq):
    T, D = h.shape
    nq = seq // bq
    once = pl.Buffered(1)
    row_spec = pl.BlockSpec((bq, D), lambda b, i: (b * nq + i, 0))
    return pl.pallas_call(
        _mem_block_kernel,
        grid=(batch, nq),
        in_specs=[
            row_spec,
            pl.BlockSpec((1, D), lambda b, i: (0, 0)),
            pl.BlockSpec((D, MEM_W), lambda b, i: (0, 0), pipeline_mode=once),
            pl.BlockSpec((mem_len, 2 * MEM_W), lambda b, i: (b, 0)),
            pl.BlockSpec((MEM_W, D), lambda b, i: (0, 0), pipeline_mode=once),
        ],
        out_specs=row_spec,
        out_shape=jax.ShapeDtypeStruct((T, D), F32),
        compiler_params=_params(("arbitrary", "arbitrary")),
        name="mem_block",
    )(h, gain.reshape(1, D).astype(F32), w_q, kv, w_o)


def _extract_max(work, rows):
    m = jnp.max(work, axis=0, keepdims=True)
    first = jnp.min(jnp.where(work == m, rows, float(work.shape[0])), axis=0, keepdims=True)
    return m, rows == first


def _top16_rows(scores, dests, work_scr, rank_scr, base):
    shape = scores[0].shape
    rows = lax.broadcasted_iota(jnp.int32, shape, 0).astype(F32)
    chains = range(len(scores))

    def run(first_only):
        for n, s in enumerate(scores):
            work_scr[base + n] = s
            rank_scr[base + n] = jnp.full(shape, float(PEER_TOPK), F32)

        def body(a, carry):
            for n in chains:
                work = work_scr[base + n]
                if first_only:
                    m, hit = _extract_max(work, rows)
                else:
                    m = jnp.max(work, axis=0, keepdims=True)
                    hit = work == m
                ref, idx = dests[n]
                ref[idx, pl.ds(a, 1), :] = m
                work_scr[base + n] = jnp.where(hit, -jnp.inf, work)
                rank_scr[base + n] = jnp.where(hit, jnp.asarray(a, F32), rank_scr[base + n])
            return carry

        lax.fori_loop(0, PEER_TOPK, body, 0)

    run(first_only=False)
    excess = jnp.zeros((1, shape[1]), F32)
    for n in chains:
        taken = jnp.sum(jnp.where(rank_scr[base + n] < float(PEER_TOPK), 1.0, 0.0), axis=0, keepdims=True)
        excess = jnp.maximum(excess, jnp.abs(taken - float(PEER_TOPK)))
    pl.when(jnp.max(excess) > 0.0)(lambda: run(first_only=True))
    return [rank_scr[base + n] for n in chains]


def _route_kernel(q_ref, keys_ref, n1_ref, e1_ref, r2_ref, e2_ref, v1_scr, v2_scr, c_scr, sel_scr, work_scr, rank_scr,
                  *, tb, group):
    half = PEER_QDIM // 2
    dn = (((1,), (1,)), ((), ()))
    s1_all = lax.dot_general(keys_ref[0, 0], q_ref[:, :half], dn, preferred_element_type=F32)
    s2_all = lax.dot_general(keys_ref[0, 1], q_ref[:, half:], dn, preferred_element_type=F32)
    n_c = tb // LANES
    cols = [slice(c * LANES, (c + 1) * LANES) for c in range(n_c)]
    s1 = [s1_all[:, cl] for cl in cols]
    s2 = [s2_all[:, cl] for cl in cols]
    rank1, rank2 = [], []
    for c0 in range(0, n_c, group):
        cs = list(range(c0, c0 + group))
        ranks = _top16_rows([s1[c] for c in cs] + [s2[c] for c in cs],
                            [(v1_scr, c) for c in cs] + [(v2_scr, c) for c in cs], work_scr, rank_scr, 2 * c0)
        rank1 += ranks[:group]
        rank2 += ranks[group:]
    for c in range(n_c):
        for p, (a, b) in enumerate(_CELLS):
            c_scr[c, p:p + 1, :] = v1_scr[c, a:a + 1, :] + v2_scr[c, b:b + 1, :]
        if _CELL_ROWS > _N_CELLS:
            c_scr[c, _N_CELLS:, :] = jnp.full((_CELL_ROWS - _N_CELLS, LANES), -jnp.inf, F32)
    cand = [c_scr[c] for c in range(n_c)]
    prow = lax.broadcasted_iota(jnp.int32, cand[0].shape, 0).astype(F32)

    def body(t, carry):
        out = []
        for work, picked in carry:
            _, hit = _extract_max(work, prow)
            out.append((jnp.where(hit, -jnp.inf, work), jnp.where(hit, 1.0, picked)))
        return tuple(out)

    picked = [pk for _, pk in lax.fori_loop(0, PEER_TOPK, body,
                                            tuple((cd, jnp.zeros(cd.shape, F32)) for cd in cand))]
    for c in range(n_c):
        top1 = v1_scr[c, 0:1, :]
        top2 = v2_scr[c, 0:1, :]
        z = jnp.sum(jnp.where(picked[c] > 0, jnp.exp(cand[c] - (top1 + top2)), 0.0), axis=0, keepdims=True)
        sel_scr[c] = picked[c]
        n1 = jnp.zeros(s1[c].shape, F32)
        p0 = 0
        for a in range(PEER_TOPK):
            width = sum(1 for (aa, _) in _CELLS if aa == a)
            n_a = jnp.sum(sel_scr[c, p0:p0 + width, :], axis=0, keepdims=True)
            n1 = jnp.where(rank1[c] == float(a), n_a, n1)
            p0 += width
        n1_ref[0, :, cols[c]] = n1
        e1_ref[0, :, cols[c]] = jnp.exp(s1[c] - top1) / z
        r2_ref[0, :, cols[c]] = rank2[c].astype(r2_ref.dtype)
        e2_ref[0, :, cols[c]] = jnp.exp(s2[c] - top2).astype(e2_ref.dtype)


def _peer_route(q, sub_keys, *, tb=512, group=4):
    T = q.shape[0]
    shape = jax.ShapeDtypeStruct((PEER_HEADS, PEER_NKEYS, T), F32)
    shape_packed = jax.ShapeDtypeStruct((PEER_HEADS, PEER_NKEYS, T), BF16)
    out_spec = pl.BlockSpec((1, PEER_NKEYS, tb), lambda i, h: (h, 0, i))
    return pl.pallas_call(
        functools.partial(_route_kernel, tb=tb, group=group),
        grid=(T // tb, PEER_HEADS),
        in_specs=[
            pl.BlockSpec((tb, PEER_QDIM), lambda i, h: (i, h)),
            pl.BlockSpec((1, 2, PEER_NKEYS, PEER_QDIM // 2), lambda i, h: (h, 0, 0, 0)),
        ],
        out_specs=[out_spec] * 4,
        out_shape=[shape, shape, shape_packed, shape_packed],
        scratch_shapes=[pltpu.VMEM((tb // LANES, PEER_TOPK, LANES), F32),
                        pltpu.VMEM((tb // LANES, PEER_TOPK, LANES), F32),
                        pltpu.VMEM((tb // LANES, _CELL_ROWS, LANES), F32),
                        pltpu.VMEM((tb // LANES, _CELL_ROWS, LANES), F32),
                        pltpu.VMEM((2 * (tb // LANES), PEER_NKEYS, LANES), F32),
                        pltpu.VMEM((2 * (tb // LANES), PEER_NKEYS, LANES), F32)],
        compiler_params=_params(("arbitrary", "arbitrary")),
        name="peer_route",
    )(q, sub_keys)


def _peer_kernel(xnt_ref, u_ref, vt_ref, n1_ref, e1_ref, r2_ref, e2_ref, o_ref, a_scr, w_scr, *, eb, tc):
    k = pl.program_id(1)

    @pl.when(k == 0)
    def _():
        o_ref[...] = jnp.zeros(o_ref.shape, F32)
        a_scr[...] = jnp.dot(u_ref[...], xnt_ref[...], preferred_element_type=F32)

    @pl.when(k > 0)
    def _():
        n_i = eb // PEER_NKEYS
        pack = 16
        groups = PEER_NKEYS // pack
        for c in range(a_scr.shape[1] // tc):
            cols = slice(c * tc, (c + 1) * tc)
            for a in range(n_i):
                rows = slice(a * PEER_NKEYS, (a + 1) * PEER_NKEYS)
                i_row = k * n_i + (a - n_i)
                a_t = a_scr[rows, cols]
                act = 0.5 * a_t * (1.0 + lax.erf(a_t * np.float32(np.sqrt(0.5))))
                gate = jnp.zeros((PEER_NKEYS, tc), BF16)
                for h in range(PEER_HEADS):
                    cnt = jnp.broadcast_to(n1_ref[h, pl.ds(i_row, 1), :][:, cols], (pack, tc)).astype(BF16)
                    g1 = jnp.broadcast_to(e1_ref[h, pl.ds(i_row, 1), :][:, cols], (pack, tc)).astype(BF16)
                    cnt = jnp.concatenate([cnt] * groups, axis=0)
                    g1 = jnp.concatenate([g1] * groups, axis=0)
                    gate = gate + jnp.where(r2_ref[h, :, cols] < cnt, e2_ref[h, :, cols] * g1,
                                            jnp.zeros((), BF16))
                w_scr[rows, cols] = (act * gate.astype(F32)).astype(BF16)
        a_scr[...] = jnp.dot(u_ref[...], xnt_ref[...], preferred_element_type=F32)
        o_ref[...] += jnp.dot(vt_ref[0], w_scr[...], preferred_element_type=F32)


def _peer_mix(xnt, u, vt, n1, e1, r2, e2, *, tb=512, tc=512):
    D, T = xnt.shape
    E = u.shape[0]
    n_e, _, eb = vt.shape
    once = pl.Buffered(1)
    route_spec = pl.BlockSpec((PEER_HEADS, PEER_NKEYS, tb), lambda i, k: (0, 0, i), pipeline_mode=once)
    return pl.pallas_call(
        functools.partial(_peer_kernel, eb=eb, tc=tc),
        grid=(T // tb, n_e + 1),
        in_specs=[
            pl.BlockSpec((D, tb), lambda i, k: (0, i), pipeline_mode=once),
            pl.BlockSpec((eb, D), lambda i, k: (jnp.minimum(k, n_e - 1), 0)),
            pl.BlockSpec((1, D, eb), lambda i, k: (jnp.maximum(k - 1, 0), 0, 0)),
            route_spec, route_spec, route_spec, route_spec,
        ],
        out_specs=pl.BlockSpec((D, tb), lambda i, k: (0, i)),
        out_shape=jax.ShapeDtypeStruct((D, T), F32),
        scratch_shapes=[pltpu.VMEM((eb, tb), F32), pltpu.VMEM((eb, tb), BF16)],
        compiler_params=_params(("arbitrary", "arbitrary")),
        name="peer_mix",
    )(xnt, u, vt, n1, e1, r2, e2)


def _transpose_cast_kernel(x_ref, o_ref):
    o_ref[0] = x_ref[...].T.astype(o_ref.dtype)


def _transpose_cast_tiles(x, rows):
    N, D = x.shape
    return pl.pallas_call(
        _transpose_cast_kernel,
        grid=(N // rows,),
        in_specs=[pl.BlockSpec((rows, D), lambda e: (e, 0))],
        out_specs=pl.BlockSpec((1, D, rows), lambda e: (e, 0, 0)),
        out_shape=jax.ShapeDtypeStruct((N // rows, D, rows), BF16),
        compiler_params=_params(("arbitrary",)),
        name="transpose_cast",
    )(x)


def _add_norm_kernel(a_ref, bt_ref, g_ref, o_ref, *, norm):
    x = a_ref[...] + bt_ref[...].T
    if norm:
        ms = jnp.mean(x * x, axis=-1, keepdims=True)
        x = (x * lax.rsqrt(ms + RMS_EPS)) * g_ref[...]
    o_ref[...] = x


def _add_norm(a, bt, gain, *, norm, bm=256):
    T, D = a.shape
    spec = pl.BlockSpec((bm, D), lambda i: (i, 0))
    return pl.pallas_call(
        functools.partial(_add_norm_kernel, norm=norm),
        grid=(T // bm,),
        in_specs=[spec, pl.BlockSpec((D, bm), lambda i: (0, i)), pl.BlockSpec((1, D), lambda i: (0, 0))],
        out_specs=spec,
        out_shape=jax.ShapeDtypeStruct((T, D), F32),
        compiler_params=_params(("arbitrary",)),
        name="add_norm",
    )(a, bt, gain.reshape(1, D).astype(F32))


def _rope_tables(seq):
    half = HEAD_DIM // 2
    inv = ROPE_THETA ** (-jnp.arange(half, dtype=F32) / half)
    ang = jnp.arange(seq, dtype=jnp.int32).astype(F32)[:, None] * inv[None, :]
    cos, sin = jnp.cos(ang), jnp.sin(ang)
    return jnp.concatenate([cos, cos], axis=-1), jnp.concatenate([-sin, sin], axis=-1)


def kernel(x, mem, norm_mix, w_in, swa_sink, nat_rel_bias, out_norm_swa, out_norm_nat, w_out, norm_mem_q,
           norm_mem_kv, w_mem_q, w_mem_k, w_mem_v, w_mem_o, norm_ffn, w_peer_q, peer_sub_keys, peer_u, peer_v,
           norm_final):
    B, S, D = x.shape
    M = mem.shape[1]
    T = B * S
    depth = w_in.shape[0]
    assert depth >= 1
    rope = _rope_tables(S)
    h = x.reshape(T, D)
    mem2 = mem.reshape(B * M, D)
    rope_cols = SWA_Q_W + SWA_KV_W
    for l in range(depth):
        qkv = _rms_matmul([h], [norm_mix[l]], w_in[l].astype(BF16), **TILES["in_proj"], out_dtype=BF16,
                          rope=rope, rope_cols=rope_cols, name="in_proj")
        oa = _swa_attention(qkv, swa_sink[l], B, S)
        ob = _nat_attention(qkv, nat_rel_bias[l], B, S)
        h = _rms_matmul([oa, ob], [out_norm_swa[l], out_norm_nat[l]], w_out[l].astype(BF16), **TILES["out_proj"],
                        out_dtype=F32, res=h, name="out_proj")
        w_kv = jnp.concatenate([w_mem_k[l], w_mem_v[l]], axis=1).astype(BF16)
        kv = _rms_matmul([mem2], [norm_mem_kv[l]], w_kv, **TILES["mem_kv_proj"], out_dtype=BF16,
                         name="mem_kv_proj")
        h = _mem_block(h, norm_mem_q[l], w_mem_q[l].astype(BF16), kv, w_mem_o[l].astype(BF16), B, S, M,
                       bq=MEM_QUERY_BLOCK)
        qp, xn = _rms_matmul([h], [norm_ffn[l]], w_peer_q[l].astype(BF16), **TILES["peer_q_proj"],
                             out_dtype=BF16, emit_xn=True, name="peer_q_proj")
        n1, e1, r2, e2 = _peer_route(qp, peer_sub_keys[l].astype(BF16))
        vt = _transpose_cast_tiles(peer_v[l], PEER_EXPERT_TILE)
        peer = _peer_mix(xn.T, peer_u[l].astype(BF16), vt, n1, e1, r2, e2)
        last = l == depth - 1
        h = _add_norm(h, peer, norm_final if last else jnp.ones((D,), F32), norm=last)
    return h.reshape(B, S, D)
```
